```python
import math
import jax, jax.numpy as jnp
from jax import lax
import numpy as np

D_MODEL = 2048
BATCH = 4
SEQ = 2048
DEPTH = 1
DEC_BATCH = 32
DEC_SEQ = 1
PAST_LEN = 8192
PAGE_SIZE = 128

D_MIX = D_MODEL
D_REC = D_MIX // 2
D_ATT = D_MIX - D_REC
N_REC_BLOCKS = 8
REC_BLOCK = D_REC // N_REC_BLOCKS
CONV_W = 4
LRU_C = 8.0
N_ATT_HEADS = 8
ATT_HEAD_DIM = D_ATT // N_ATT_HEADS
QK_DIM = ATT_HEAD_DIM // 2
ROPE_THETA = 10000.0
Q_BLOCK = 128
N_KEYS = 128
N_EXPERTS = N_KEYS * N_KEYS
PEER_HEADS = 8
PEER_KEY_DIM = 256
PEER_HALF = PEER_KEY_DIM // 2
PEER_TOPK = 16
TOKEN_BLOCK = 128
LN_EPS = 1e-5
NEG_INF = -1e30

kernel_name = 'hymba_rglru_diffattn_peer_step'


def layer_norm(x, g, b):
    x32 = x.astype(jnp.float32)
    mu = jnp.mean(x32, axis=-1, keepdims=True)
    xc = x32 - mu
    var = jnp.mean(xc * xc, axis=-1, keepdims=True)
    return (xc * lax.rsqrt(var + LN_EPS) * g.astype(jnp.float32) + b.astype(jnp.float32)).astype(x.dtype)


def rms_norm(x, g):
    x32 = x.astype(jnp.float32)
    return x32 * lax.rsqrt(jnp.mean(x32 * x32, axis=-1, keepdims=True) + LN_EPS) * g.astype(jnp.float32)


def rope(x, pos):
    dim = x.shape[-1]
    half = dim // 2
    inv = ROPE_THETA ** (-jnp.arange(half, dtype=jnp.float32) * 2.0 / dim)
    ang = pos.astype(jnp.float32)[:, None] * inv[None, :]
    cos = jnp.cos(ang)[:, None, :]
    sin = jnp.sin(ang)[:, None, :]
    x32 = x.astype(jnp.float32)
    x1, x2 = x32[..., :half], x32[..., half:]
    return jnp.concatenate([x1 * cos - x2 * sin, x2 * cos + x1 * sin], axis=-1).astype(x.dtype)


def causal_conv(x, buf, w, b):
    xp = jnp.concatenate([buf.astype(jnp.float32), x.astype(jnp.float32)], axis=1)
    t = x.shape[1]
    w32 = w.astype(jnp.float32)
    out = b.astype(jnp.float32) + sum(xp[:, j:j + t] * w32[j] for j in range(CONV_W))
    return out, xp[:, -(CONV_W - 1):]


def rg_lru(x, h0, w_a, b_a, w_x, b_x, lam):
    bsz, t, _ = x.shape
    xb = x.reshape(bsz, t, N_REC_BLOCKS, REC_BLOCK)
    r = jax.nn.sigmoid(jnp.einsum('btni,nij->btnj', xb, w_a.astype(jnp.float32)).reshape(bsz, t, D_REC) + b_a.astype(jnp.float32))
    i = jax.nn.sigmoid(jnp.einsum('btni,nij->btnj', xb, w_x.astype(jnp.float32)).reshape(bsz, t, D_REC) + b_x.astype(jnp.float32))
    log_a = -LRU_C * r * jax.nn.softplus(-lam.astype(jnp.float32))
    a = jnp.exp(log_a)
    bterm = jnp.sqrt(-jnp.expm1(2.0 * log_a)) * (i * x)
    bterm = bterm.at[:, 0].add(a[:, 0] * h0.astype(jnp.float32))

    def combine(c1, c2):
        a1, b1 = c1
        a2, b2 = c2
        return a1 * a2, a2 * b1 + b2

    _, h = lax.associative_scan(combine, (a, bterm), axis=1)
    return h, h[:, -1]


def diff_attn_block(q, k, v, q_pos, k_pos, lam):
    bsz, tq = q.shape[:2]
    tk = k.shape[1]
    qh = q.reshape(bsz, tq, N_ATT_HEADS, 2, QK_DIM)
    kh = k.reshape(bsz, tk, N_ATT_HEADS, 2, QK_DIM)
    s = jnp.einsum('bqhcd,bkhcd->bhcqk', qh, kh, preferred_element_type=jnp.float32) * (QK_DIM ** -0.5)
    mask = k_pos[None, :] <= q_pos[:, None]
    s = jnp.where(mask, s, NEG_INF)
    p = jax.nn.softmax(s, axis=-1)
    w = p[:, :, 0] - lam * p[:, :, 1]
    return jnp.einsum('bhqk,bkhd->bqhd', w, v.astype(jnp.float32))


def diff_attention(q, k, v, q_pos, k_pos, lam):
    bsz, t = q.shape[:2]
    if t <= Q_BLOCK or t % Q_BLOCK != 0:
        return diff_attn_block(q, k, v, q_pos, k_pos, lam)
    nb = t // Q_BLOCK
    qb = q.reshape(bsz, nb, Q_BLOCK, q.shape[2], q.shape[3]).swapaxes(0, 1)
    pb = q_pos.reshape(nb, Q_BLOCK)
    ob = lax.map(lambda a: diff_attn_block(a[0], k, v, a[1], k_pos, lam), (qb, pb))
    return ob.swapaxes(0, 1).reshape(bsz, t, N_ATT_HEADS, ATT_HEAD_DIM)


def peer_block(xb, wq, sub_k1, sub_k2, u_tab, v_tab):
    n = xb.shape[0]
    q = jnp.dot(xb, wq, preferred_element_type=jnp.float32).reshape(n, PEER_HEADS, 2, PEER_HALF)
    s1 = jnp.einsum('nhd,hkd->nhk', q[:, :, 0], sub_k1.astype(jnp.float32))
    s2 = jnp.einsum('nhd,hkd->nhk', q[:, :, 1], sub_k2.astype(jnp.float32))
    v1, i1 = lax.top_k(s1, PEER_TOPK)
    v2, i2 = lax.top_k(s2, PEER_TOPK)
    comb = (v1[..., :, None] + v2[..., None, :]).reshape(n, PEER_HEADS, PEER_TOPK * PEER_TOPK)
    sv, si = lax.top_k(comb, PEER_TOPK)
    e = (jnp.take_along_axis(i1, si // PEER_TOPK, axis=-1) * N_KEYS
         + jnp.take_along_axis(i2, si % PEER_TOPK, axis=-1))
    g = jax.nn.softmax(sv, axis=-1)
    ug = u_tab[e]
    act = jax.nn.gelu(jnp.einsum('nd,nhkd->nhk', xb, ug, preferred_element_type=jnp.float32))
    vg = v_tab[e]
    out = jnp.einsum('nhk,nhkd->nd', (g * act).astype(vg.dtype), vg, preferred_element_type=jnp.float32)
    return out.astype(xb.dtype)


def peer(x, wq, sub_k1, sub_k2, u_tab, v_tab):
    bsz, t, d = x.shape
    n = bsz * t
    n_pad = (-n) % TOKEN_BLOCK
    xt = jnp.pad(x.reshape(n, d), ((0, n_pad), (0, 0)))
    xblk = xt.reshape(-1, TOKEN_BLOCK, d)
    out = lax.map(lambda xb: peer_block(xb, wq, sub_k1, sub_k2, u_tab, v_tab), xblk)
    return out.reshape(-1, d)[:n].reshape(bsz, t, d)


def layer(x, q_pos, k_past, v_past, conv_buf, h0, lam_init, p):
    bsz, t, _ = x.shape
    alpha = (2.0 * DEPTH) ** 0.25
    proj = x @ p['w_in']
    rec_x, rec_g, q, k, v = jnp.split(proj, [D_REC, 2 * D_REC, 2 * D_REC + D_ATT, 2 * D_REC + 2 * D_ATT], axis=-1)
    conv_out, new_buf = causal_conv(rec_x, conv_buf, p['conv_w'], p['conv_b'])
    hs, h_last = rg_lru(conv_out, h0, p['lru_wa'], p['lru_ba'], p['lru_wx'], p['lru_bx'], p['lru_lambda'])
    rec_out = hs * jax.nn.gelu(rec_g.astype(jnp.float32))
    q = rope(q.reshape(bsz, t, 2 * N_ATT_HEADS, QK_DIM), q_pos)
    k = rope(k.reshape(bsz, t, 2 * N_ATT_HEADS, QK_DIM), q_pos)
    v = v.reshape(bsz, t, N_ATT_HEADS, ATT_HEAD_DIM)
    if k_past is None:
        k_all, v_all, k_pos = k, v, q_pos
    else:
        k_all = jnp.concatenate([k_past.astype(k.dtype), k], axis=1)
        v_all = jnp.concatenate([v_past.astype(v.dtype), v], axis=1)
        k_pos = jnp.arange(k_all.shape[1], dtype=jnp.int32)
    f32 = jnp.float32
    lam = (jnp.exp(jnp.sum(p['lambda_q1'].astype(f32) * p['lambda_k1'].astype(f32)))
           - jnp.exp(jnp.sum(p['lambda_q2'].astype(f32) * p['lambda_k2'].astype(f32))) + lam_init)
    att = diff_attention(q, k_all, v_all, q_pos, k_pos, lam)
    att = (rms_norm(att, p['subln_g']) * (1.0 - lam_init)).reshape(bsz, t, D_ATT)
    mix = jnp.concatenate([rec_out, att], axis=-1).astype(x.dtype) @ p['w_out']
    x1 = layer_norm(alpha * x + mix, p['ln1_g'], p['ln1_b'])
    ff = peer(x1, p['peer_wq'], p['peer_k1'], p['peer_k2'], p['peer_u'], p['peer_v'])
    x2 = layer_norm(alpha * x1 + ff, p['ln2_g'], p['ln2_b'])
    return x2, k, v, new_buf.astype(x.dtype), h_last.astype(x.dtype)


def setup_inputs(seed: int = 0) -> dict:
    key = jax.random.key(seed)
    ks = jax.random.split(key, 32)
    f32 = jnp.float32
    n_pages = PAST_LEN // PAGE_SIZE
    n_pool = (DEC_BATCH * n_pages * 5 + 3) // 4
    beta = (8.0 * DEPTH) ** -0.25

    def nrm(k, shape, s):
        return jax.random.normal(k, shape, f32) * s

    in_cols = 2 * D_REC + 3 * D_ATT
    col_scale = jnp.concatenate([jnp.ones((2 * D_REC + 2 * D_ATT,), f32), jnp.full((D_ATT,), beta, f32)])
    a0 = jax.random.uniform(ks[10], (DEPTH, D_REC), f32, minval=0.9, maxval=0.999)
    sa = a0 ** (1.0 / LRU_C)
    perm = jax.random.permutation(ks[31], n_pool)[:DEC_BATCH * n_pages]
    return {
        'x_prompt': nrm(ks[0], (BATCH, SEQ, D_MODEL), 1.0),
        'x_sample': nrm(ks[1], (DEC_BATCH, DEC_SEQ, D_MODEL), 1.0),
        'cache_k': nrm(ks[2], (DEPTH, n_pool, PAGE_SIZE, 2 * N_ATT_HEADS, QK_DIM), 1.0),
        'cache_v': nrm(ks[3], (DEPTH, n_pool, PAGE_SIZE, N_ATT_HEADS, ATT_HEAD_DIM), 1.0),
        'state_conv': nrm(ks[4], (DEPTH, DEC_BATCH, CONV_W - 1, D_REC), 1.0),
        'state_h': nrm(ks[5], (DEPTH, DEC_BATCH, D_REC), 0.5),
        'page_table': perm.reshape(DEC_BATCH, n_pages).astype(jnp.int32),
        'w_in': nrm(ks[6], (DEPTH, D_MODEL, in_cols), D_MODEL ** -0.5) * col_scale,
        'conv_w': nrm(ks[7], (DEPTH, CONV_W, D_REC), CONV_W ** -0.5),
        'conv_b': nrm(ks[8], (DEPTH, D_REC), 0.01),
        'lru_wa': nrm(ks[9], (DEPTH, N_REC_BLOCKS, REC_BLOCK, REC_BLOCK), REC_BLOCK ** -0.5),
        'lru_ba': nrm(ks[11], (DEPTH, D_REC), 0.01),
        'lru_wx': nrm(ks[12], (DEPTH, N_REC_BLOCKS, REC_BLOCK, REC_BLOCK), REC_BLOCK ** -0.5),
        'lru_bx': nrm(ks[13], (DEPTH, D_REC), 0.01),
        'lru_lambda': jnp.log(sa) - jnp.log1p(-sa),
        'lambda_q1': nrm(ks[14], (DEPTH, QK_DIM), 0.1),
        'lambda_k1': nrm(ks[15], (DEPTH, QK_DIM), 0.1),
        'lambda_q2': nrm(ks[16], (DEPTH, QK_DIM), 0.1),
        'lambda_k2': nrm(ks[17], (DEPTH, QK_DIM), 0.1),
        'subln_g': 1.0 + nrm(ks[18], (DEPTH, ATT_HEAD_DIM), 0.02),
        'w_out': nrm(ks[19], (DEPTH, D_MIX, D_MODEL), beta * D_MIX ** -0.5),
        'ln1_g': 1.0 + nrm(ks[20], (DEPTH, D_MODEL), 0.02),
        'ln1_b': nrm(ks[21], (DEPTH, D_MODEL), 0.01),
        'peer_wq': nrm(ks[22], (DEPTH, D_MODEL, PEER_HEADS * PEER_KEY_DIM), D_MODEL ** -0.5),
        'peer_k1': nrm(ks[23], (DEPTH, PEER_HEADS, N_KEYS, PEER_HALF), PEER_HALF ** -0.5),
        'peer_k2': nrm(ks[24], (DEPTH, PEER_HEADS, N_KEYS, PEER_HALF), PEER_HALF ** -0.5),
        'peer_u': nrm(ks[25], (DEPTH, N_EXPERTS, D_MODEL), D_MODEL ** -0.5),
        'peer_v': nrm(ks[26], (DEPTH, N_EXPERTS, D_MODEL), beta * PEER_HEADS ** -0.5),
        'ln2_g': 1.0 + nrm(ks[27], (DEPTH, D_MODEL), 0.02),
        'ln2_b': nrm(ks[28], (DEPTH, D_MODEL), 0.01),
    }


def reference(x_prompt, x_sample, cache_k, cache_v, state_conv, state_h, page_table,
              w_in, conv_w, conv_b, lru_wa, lru_ba, lru_wx, lru_bx, lru_lambda,
              lambda_q1, lambda_k1, lambda_q2, lambda_k2, subln_g, w_out, ln1_g, ln1_b,
              peer_wq, peer_k1, peer_k2, peer_u, peer_v, ln2_g, ln2_b):
    bsz, seq, _ = x_prompt.shape
    dbsz, dseq, _ = x_sample.shape
    n_pages = page_table.shape[1]
    page = cache_k.shape[2]
    past = n_pages * page
    pos_p = jnp.arange(seq, dtype=jnp.int32)
    pos_s = past + jnp.arange(dseq, dtype=jnp.int32)
    yp, ys = x_prompt, x_sample
    kp_l, vp_l, cp_l, hp_l, ks_l, vs_l, cs_l, hs_l = [], [], [], [], [], [], [], []
    for l in range(DEPTH):
        p = dict(w_in=w_in[l], conv_w=conv_w[l], conv_b=conv_b[l], lru_wa=lru_wa[l], lru_ba=lru_ba[l],
                 lru_wx=lru_wx[l], lru_bx=lru_bx[l], lru_lambda=lru_lambda[l],
                 lambda_q1=lambda_q1[l], lambda_k1=lambda_k1[l], lambda_q2=lambda_q2[l], lambda_k2=lambda_k2[l],
                 subln_g=subln_g[l], w_out=w_out[l], ln1_g=ln1_g[l], ln1_b=ln1_b[l],
                 peer_wq=peer_wq[l], peer_k1=peer_k1[l], peer_k2=peer_k2[l], peer_u=peer_u[l], peer_v=peer_v[l],
                 ln2_g=ln2_g[l], ln2_b=ln2_b[l])
        lam_init = 0.8 - 0.6 * math.exp(-0.3 * l)
        conv0 = jnp.zeros((bsz, CONV_W - 1, D_REC), x_prompt.dtype)
        h0 = jnp.zeros((bsz, D_REC), x_prompt.dtype)
        yp, kp, vp, cp, hp = layer(yp, pos_p, None, None, conv0, h0, lam_init, p)
        k_past = cache_k[l][page_table].reshape(dbsz, past, 2 * N_ATT_HEADS, QK_DIM)
        v_past = cache_v[l][page_table].reshape(dbsz, past, N_ATT_HEADS, ATT_HEAD_DIM)
        ys, ksn, vsn, csn, hsn = layer(ys, pos_s, k_past, v_past, state_conv[l], state_h[l], lam_init, p)
        kp_l.append(kp); vp_l.append(vp); cp_l.append(cp); hp_l.append(hp)
        ks_l.append(ksn); vs_l.append(vsn); cs_l.append(csn); hs_l.append(hsn)
    k_prompt = jnp.stack(kp_l)
    v_prompt = jnp.stack(vp_l)
    conv_prompt = jnp.stack(cp_l)
    h_prompt = jnp.stack(hp_l)
    k_sample = jnp.stack(ks_l)
    v_sample = jnp.stack(vs_l)
    conv_sample = jnp.stack(cs_l)
    h_sample = jnp.stack(hs_l)
    return (yp, ys, k_prompt, v_prompt, conv_prompt, h_prompt, k_sample, v_sample, conv_sample, h_sample)
```

```python
import functools
import math

import jax
import jax.numpy as jnp
from jax import lax
from jax.experimental import pallas as pl
from jax.experimental.pallas import tpu as pltpu

F32 = jnp.float32
BF16 = jnp.bfloat16

D_MODEL = 2048
D_REC = 1024
D_ATT = 1024
N_REC_BLOCKS = 8
REC_BLOCK = 128
CONV_W = 4
LRU_C = 8.0
N_ATT_HEADS = 8
ATT_HEAD_DIM = 128
QK_DIM = 64
ROPE_THETA = 10000.0
N_KEYS = 128
N_EXPERTS = N_KEYS * N_KEYS
PEER_HEADS = 8
PEER_TOPK = 16
LN_EPS = 1e-5
NEG_INF = -1e30
DEPTH = 1
ALPHA = (2.0 * DEPTH) ** 0.25
LAM_INIT = 0.8 - 0.6 * math.exp(-0.3 * 0)
LANES = 128

VMEM_LIMIT = 56 * 1024 * 1024


def _cparams(sem):
    return pltpu.CompilerParams(dimension_semantics=sem, vmem_limit_bytes=VMEM_LIMIT)


def _gelu(x):
    c = math.sqrt(2.0 / math.pi)
    return x * (0.5 * (1.0 + jnp.tanh(c * (x + 0.044715 * (x * x * x)))))


def _layer_norm(y, g, b):
    mu = jnp.mean(y, axis=-1, keepdims=True)
    yc = y - mu
    var = jnp.mean(yc * yc, axis=-1, keepdims=True)
    return yc * lax.rsqrt(var + LN_EPS) * g + b


def _proj_kernel(x_ref, w_ref, cos_ref, sin_ref, o_ref, *, rope):
    acc = jnp.dot(x_ref[...], w_ref[...], preferred_element_type=F32)
    if rope:
        c = cos_ref[...]
        s = sin_ref[...]
        lane = lax.broadcasted_iota(jnp.int32, (1, LANES), 1)
        first_half = (lane % QK_DIM) < (QK_DIM // 2)
        outs = []
        for j in range(acc.shape[1] // LANES):
            xc = acc[:, j * LANES:(j + 1) * LANES]
            fwd = pltpu.roll(xc, LANES - QK_DIM // 2, axis=1)
            bwd = pltpu.roll(xc, QK_DIM // 2, axis=1)
            partner = jnp.where(first_half, fwd, bwd)
            outs.append(xc * c + partner * s)
        acc = jnp.concatenate(outs, axis=1)
    o_ref[...] = acc.astype(o_ref.dtype)


def _proj(x_bf, w_bf, cos_t, sin_t, *, rope, out_dtype, tm, tn):
    n, k = x_bf.shape
    m = w_bf.shape[1]
    n_pos_blocks = cos_t.shape[0] // tm
    return pl.pallas_call(
        functools.partial(_proj_kernel, rope=rope),
        grid=(n // tm, m // tn),
        in_specs=[
            pl.BlockSpec((tm, k), lambda i, j: (i, 0)),
            pl.BlockSpec((k, tn), lambda i, j: (0, j)),
            pl.BlockSpec((tm, LANES), lambda i, j: (i % n_pos_blocks, 0)),
            pl.BlockSpec((tm, LANES), lambda i, j: (i % n_pos_blocks, 0)),
        ],
        out_specs=pl.BlockSpec((tm, tn), lambda i, j: (i, j)),
        out_shape=jax.ShapeDtypeStruct((n, m), out_dtype),
        compiler_params=_cparams(("parallel", "parallel")),
        name="in_proj_rope" if rope else "in_proj",
    )(x_bf, w_bf, cos_t, sin_t)


def _lru_gates(conv, wa_ref, wx_ref, ba, bx, lam):
    rs, is_ = [], []
    for blk in range(conv.shape[1] // REC_BLOCK):
        cb = conv[:, blk * REC_BLOCK:(blk + 1) * REC_BLOCK].astype(BF16)
        rs.append(jnp.dot(cb, wa_ref[blk], preferred_element_type=F32))
        is_.append(jnp.dot(cb, wx_ref[blk], preferred_element_type=F32))
    r = jax.nn.sigmoid(jnp.concatenate(rs, axis=1) + ba)
    i = jax.nn.sigmoid(jnp.concatenate(is_, axis=1) + bx)
    softplus_neg_lam = jnp.maximum(-lam, 0.0) + jnp.log1p(jnp.exp(-jnp.abs(lam)))
    log_a = -LRU_C * r * softplus_neg_lam
    a = jnp.exp(log_a)
    b = jnp.sqrt(1.0 - jnp.exp(2.0 * log_a)) * (i * conv)
    return a, b


def _shift_rows(x, s, fill):
    row = lax.broadcasted_iota(jnp.int32, x.shape, 0)
    return jnp.where(row >= s, pltpu.roll(x, s, axis=0), fill)


def _rec_prompt_kernel(x_ref, g_ref, cw_ref, cb_ref, wa_ref, wx_ref, ba_ref, bx_ref, lam_ref,
                       o_ref, h_ref):
    x = x_ref[...]
    t = x.shape[0]
    cw = cw_ref[...]
    conv = (cb_ref[...] + cw[3:4] * x + cw[2:3] * _shift_rows(x, 1, 0.0)
            + cw[1:2] * _shift_rows(x, 2, 0.0) + cw[0:1] * _shift_rows(x, 3, 0.0))
    a, b = _lru_gates(conv, wa_ref, wx_ref, ba_ref[...], bx_ref[...], lam_ref[...])
    s = 1
    while s < t:
        b = a * _shift_rows(b, s, 0.0) + b
        a = a * _shift_rows(a, s, 1.0)
        s *= 2
    o_ref[...] = (b * _gelu(g_ref[...])).astype(o_ref.dtype)
    h_ref[0] = b[t - 1:t, :]


def _rec_prompt(rec, conv_w, conv_b, wa_bf, wx_bf, ba, bx, lam, *, bsz, t, cb):
    ncb = D_REC // cb
    return pl.pallas_call(
        _rec_prompt_kernel,
        grid=(bsz, ncb),
        in_specs=[
            pl.BlockSpec((t, cb), lambda b, c: (b, c)),
            pl.BlockSpec((t, cb), lambda b, c: (b, ncb + c)),
            pl.BlockSpec((CONV_W, cb), lambda b, c: (0, c)),
            pl.BlockSpec((1, cb), lambda b, c: (0, c)),
            pl.BlockSpec((cb // REC_BLOCK, REC_BLOCK, REC_BLOCK), lambda b, c: (c, 0, 0)),
            pl.BlockSpec((cb // REC_BLOCK, REC_BLOCK, REC_BLOCK), lambda b, c: (c, 0, 0)),
            pl.BlockSpec((1, cb), lambda b, c: (0, c)),
            pl.BlockSpec((1, cb), lambda b, c: (0, c)),
            pl.BlockSpec((1, cb), lambda b, c: (0, c)),
        ],
        out_specs=[
            pl.BlockSpec((t, cb), lambda b, c: (b, c)),
            pl.BlockSpec((1, 1, cb), lambda b, c: (b, 0, c)),
        ],
        out_shape=[
            jax.ShapeDtypeStruct((bsz * t, D_REC), BF16),
            jax.ShapeDtypeStruct((bsz, 1, D_REC), F32),
        ],
        compiler_params=_cparams(("parallel", "parallel")),
        name="rglru_prompt",
    )(rec, rec, conv_w, conv_b, wa_bf, wx_bf, ba, bx, lam)


def _rec_sample_kernel(rec_ref, sc_ref, h0_ref, cw_ref, cb_ref, wa_ref, wx_ref, ba_ref, bx_ref,
                       lam_ref, o_ref, nc_ref, h_ref):
    x = rec_ref[:, :D_REC]
    g = rec_ref[:, D_REC:]
    cw = cw_ref[...]
    conv = (cb_ref[...] + cw[0:1] * sc_ref[0] + cw[1:2] * sc_ref[1] + cw[2:3] * sc_ref[2]
            + cw[3:4] * x)
    a, b = _lru_gates(conv, wa_ref, wx_ref, ba_ref[...], bx_ref[...], lam_ref[...])
    h = a * h0_ref[...] + b
    o_ref[...] = (h * _gelu(g)).astype(o_ref.dtype)
    nc_ref[0] = sc_ref[1]
    nc_ref[1] = sc_ref[2]
    nc_ref[2] = x
    h_ref[...] = h


def _rec_sample(rec, sc_t, h0, conv_w, conv_b, wa_bf, wx_bf, ba, bx, lam):
    n = rec.shape[0]
    return pl.pallas_call(
        _rec_sample_kernel,
        out_shape=[
            jax.ShapeDtypeStruct((n, D_REC), BF16),
            jax.ShapeDtypeStruct((CONV_W - 1, n, D_REC), F32),
            jax.ShapeDtypeStruct((n, D_REC), F32),
        ],
        compiler_params=pltpu.CompilerParams(vmem_limit_bytes=VMEM_LIMIT),
        name="rglru_sample",
    )(rec, sc_t, h0, conv_w, conv_b, wa_bf, wx_bf, ba, bx, lam)


def _diff_lambda(lp):
    d1 = jnp.sum(lp[0:1] * lp[1:2], axis=1, keepdims=True)
    d2 = jnp.sum(lp[2:3] * lp[3:4], axis=1, keepdims=True)
    return jnp.exp(d1) - jnp.exp(d2) + LAM_INIT


def _nt_dot(a, b):
    return lax.dot_general(a, b, (((1,), (1,)), ((), ())), preferred_element_type=F32)


def _attn_prompt_kernel(q_ref, k_ref, v_ref, lp_ref, g_ref, o_ref, *, tq):
    qi = pl.program_id(2)
    q = q_ref[...]
    lane = lax.broadcasted_iota(jnp.int32, (1, LANES), 1)
    scale = QK_DIM ** -0.5
    qs = (q * scale).astype(BF16)
    zero = jnp.zeros_like(qs)
    q1 = jnp.where(lane < QK_DIM, qs, zero)
    q2 = jnp.where(lane >= QK_DIM, qs, zero)

    def block(kb, carry, masked):
        m1, l1, a1, m2, l2, a2 = carry
        start = pl.multiple_of(kb * tq, tq)
        k = k_ref[pl.ds(start, tq), :].astype(BF16)
        v = v_ref[pl.ds(start, tq), :].astype(BF16)

        def one(qc, m, l, acc):
            s = _nt_dot(qc, k)
            if masked:
                row = lax.broadcasted_iota(jnp.int32, s.shape, 0)
                col = lax.broadcasted_iota(jnp.int32, s.shape, 1)
                s = jnp.where(col <= row, s, NEG_INF)
            m_new = jnp.maximum(m, jnp.max(s, axis=1, keepdims=True))
            alpha = jnp.exp(m - m_new)
            p = jnp.exp(s - m_new)
            l_new = alpha * l + jnp.sum(p, axis=1, keepdims=True)
            acc_new = alpha * acc + jnp.dot(p.astype(BF16), v, preferred_element_type=F32)
            return m_new, l_new, acc_new

        m1, l1, a1 = one(q1, m1, l1, a1)
        m2, l2, a2 = one(q2, m2, l2, a2)
        return m1, l1, a1, m2, l2, a2

    m0 = jnp.full((tq, 1), NEG_INF, F32)
    l0 = jnp.zeros((tq, 1), F32)
    a0 = jnp.zeros((tq, ATT_HEAD_DIM), F32)
    carry = (m0, l0, a0, m0, l0, a0)
    carry = lax.fori_loop(0, qi, lambda kb, c: block(kb, c, False), carry)
    m1, l1, a1, m2, l2, a2 = block(qi, carry, True)
    lam = _diff_lambda(lp_ref[...])
    att = a1 / l1 - lam * (a2 / l2)
    ms = jnp.mean(att * att, axis=1, keepdims=True)
    att = att * lax.rsqrt(ms + LN_EPS) * g_ref[...] * (1.0 - LAM_INIT)
    o_ref[...] = att.astype(o_ref.dtype)


def _attn_prompt(q_bf, k, v, lam_p, subln_g, *, bsz, t, tq):
    nq = t // tq
    return pl.pallas_call(
        functools.partial(_attn_prompt_kernel, tq=tq),
        grid=(bsz, N_ATT_HEADS, nq),
        in_specs=[
            pl.BlockSpec((tq, LANES), lambda b, h, i: (b * nq + i, h)),
            pl.BlockSpec((t, LANES), lambda b, h, i: (b, h)),
            pl.BlockSpec((t, LANES), lambda b, h, i: (b, h)),
            pl.BlockSpec((4, QK_DIM), lambda b, h, i: (0, 0)),
            pl.BlockSpec((1, ATT_HEAD_DIM), lambda b, h, i: (0, 0)),
        ],
        out_specs=pl.BlockSpec((tq, LANES), lambda b, h, i: (b * nq + i, h)),
        out_shape=jax.ShapeDtypeStruct((bsz * t, D_ATT), BF16),
        compiler_params=_cparams(("parallel", "parallel", "arbitrary")),
        name="diff_attn_prompt",
    )(q_bf, k, v, lam_p, subln_g)


N_MAPS = 2 * N_ATT_HEADS


def _attn_sample_kernel(pt_ref, q_ref, kn_ref, vn_ref, k_ref, v_ref, lp_ref, g_ref, o_ref,
                        m_sc, l_sc, acc_sc, *, n_pages):
    del pt_ref
    p_idx = pl.program_id(1)
    row = lax.broadcasted_iota(jnp.int32, (N_MAPS, D_ATT), 0)
    col = lax.broadcasted_iota(jnp.int32, (N_MAPS, D_ATT), 1)
    q_slot = col // QK_DIM == 2 * (row % N_ATT_HEADS) + row // N_ATT_HEADS
    v_slot = col // ATT_HEAD_DIM == row % N_ATT_HEADS
    scale = QK_DIM ** -0.5
    q_bd = jnp.where(q_slot, q_ref[0] * scale, 0.0)

    @pl.when(p_idx == 0)
    def _():
        s_self = jnp.sum(q_bd * kn_ref[0], axis=1, keepdims=True)
        m_sc[...] = jnp.broadcast_to(s_self, m_sc.shape)
        l_sc[...] = jnp.ones_like(l_sc)
        acc_sc[...] = jnp.broadcast_to(vn_ref[0], acc_sc.shape)

    k = k_ref[0].astype(BF16)
    v = v_ref[0].astype(BF16)
    s = _nt_dot(q_bd.astype(BF16), k)
    m_old = m_sc[...]
    m_new = jnp.maximum(m_old, jnp.max(s, axis=1, keepdims=True))
    alpha = jnp.exp(m_old - m_new)
    p = jnp.exp(s - m_new[:, 0:1])
    l_sc[...] = alpha * l_sc[...] + jnp.sum(p, axis=1, keepdims=True)
    acc_sc[...] = alpha[:, 0:1] * acc_sc[...] + jnp.dot(p.astype(BF16), v,
                                                        preferred_element_type=F32)
    m_sc[...] = m_new

    @pl.when(p_idx == n_pages - 1)
    def _():
        lam = _diff_lambda(lp_ref[...])
        o = acc_sc[...] / l_sc[:, 0:1]
        att = jnp.where(v_slot[:N_ATT_HEADS], o[:N_ATT_HEADS] - lam * o[N_ATT_HEADS:], 0.0)
        ms = jnp.sum(att * att, axis=1, keepdims=True) * (1.0 / ATT_HEAD_DIM)
        att = att * lax.rsqrt(ms + LN_EPS)
        att = jnp.sum(att, axis=0, keepdims=True) * g_ref[...] * (1.0 - LAM_INIT)
        o_ref[0] = att.astype(o_ref.dtype)


def _attn_sample(page_table, q, k_new, v_new, cache_k, cache_v, lam_p, g_tiled):
    nb, n_pages = page_table.shape
    n_pool, page = cache_k.shape[0], cache_k.shape[1]
    grid_spec = pltpu.PrefetchScalarGridSpec(
        num_scalar_prefetch=1,
        grid=(nb, n_pages),
        in_specs=[
            pl.BlockSpec((1, 1, D_ATT), lambda b, p, pt: (b, 0, 0)),
            pl.BlockSpec((1, 1, D_ATT), lambda b, p, pt: (b, 0, 0)),
            pl.BlockSpec((1, 1, D_ATT), lambda b, p, pt: (b, 0, 0)),
            pl.BlockSpec((1, page, D_ATT), lambda b, p, pt: (pt[b, p], 0, 0)),
            pl.BlockSpec((1, page, D_ATT), lambda b, p, pt: (pt[b, p], 0, 0)),
            pl.BlockSpec((4, QK_DIM), lambda b, p, pt: (0, 0)),
            pl.BlockSpec((1, D_ATT), lambda b, p, pt: (0, 0)),
        ],
        out_specs=pl.BlockSpec((1, 1, D_ATT), lambda b, p, pt: (b, 0, 0)),
        scratch_shapes=[
            pltpu.VMEM((N_MAPS, LANES), F32),
            pltpu.VMEM((N_MAPS, LANES), F32),
            pltpu.VMEM((N_MAPS, D_ATT), F32),
        ],
    )
    del n_pool
    return pl.pallas_call(
        functools.partial(_attn_sample_kernel, n_pages=n_pages),
        grid_spec=grid_spec,
        out_shape=jax.ShapeDtypeStruct((nb, 1, D_ATT), F32),
        compiler_params=_cparams(("parallel", "arbitrary")),
        name="diff_attn_sample",
    )(page_table, q, k_new, v_new, cache_k, cache_v, lam_p, g_tiled)


def _outproj_kernel(rec_ref, att_ref, wr_ref, wa_ref, x_ref, g_ref, b_ref, o_ref, ob_ref):
    mix = jnp.dot(rec_ref[...], wr_ref[...], preferred_element_type=F32)
    mix = mix + jnp.dot(att_ref[...], wa_ref[...], preferred_element_type=F32)
    y = _layer_norm(ALPHA * x_ref[...] + mix, g_ref[...], b_ref[...])
    o_ref[...] = y
    ob_ref[...] = y.astype(BF16)


def _outproj(rec_o, att_o, w_rec, w_att, x, g, b, *, tm):
    n = x.shape[0]
    return pl.pallas_call(
        _outproj_kernel,
        grid=(n // tm,),
        in_specs=[
            pl.BlockSpec((tm, D_REC), lambda i: (i, 0)),
            pl.BlockSpec((tm, D_ATT), lambda i: (i, 0)),
            pl.BlockSpec((D_REC, D_MODEL), lambda i: (0, 0)),
            pl.BlockSpec((D_ATT, D_MODEL), lambda i: (0, 0)),
            pl.BlockSpec((tm, D_MODEL), lambda i: (i, 0)),
            pl.BlockSpec((1, D_MODEL), lambda i: (0, 0)),
            pl.BlockSpec((1, D_MODEL), lambda i: (0, 0)),
        ],
        out_specs=[
            pl.BlockSpec((tm, D_MODEL), lambda i: (i, 0)),
            pl.BlockSpec((tm, D_MODEL), lambda i: (i, 0)),
        ],
        out_shape=[
            jax.ShapeDtypeStruct((n, D_MODEL), F32),
            jax.ShapeDtypeStruct((n, D_MODEL), BF16),
        ],
        compiler_params=_cparams(("parallel",)),
        name="out_proj_ln",
    )(rec_o, att_o, w_rec, w_att, x, g, b)


def _peer_scores_kernel(x_ref, wq_ref, k1_ref, k2_ref, o_ref):
    q = jnp.dot(x_ref[...], wq_ref[...], preferred_element_type=F32).astype(BF16)
    for h in range(PEER_HEADS):
        for c, k_ref in enumerate((k1_ref, k2_ref)):
            qb = q[:, (2 * h + c) * N_KEYS:(2 * h + c + 1) * N_KEYS]
            base = (c * PEER_HEADS + h) * N_KEYS
            o_ref[base:base + N_KEYS, :] = _nt_dot(k_ref[h], qb)


def _peer_scores(x1_bf, wq_bf, k1_bf, k2_bf, *, tm):
    n = x1_bf.shape[0]
    rows = 2 * PEER_HEADS * N_KEYS
    return pl.pallas_call(
        _peer_scores_kernel,
        grid=(n // tm,),
        in_specs=[
            pl.BlockSpec((tm, D_MODEL), lambda i: (i, 0)),
            pl.BlockSpec((D_MODEL, rows), lambda i: (0, 0)),
            pl.BlockSpec((PEER_HEADS, N_KEYS, N_KEYS), lambda i: (0, 0, 0)),
            pl.BlockSpec((PEER_HEADS, N_KEYS, N_KEYS), lambda i: (0, 0, 0)),
        ],
        out_specs=pl.BlockSpec((rows, tm), lambda i: (0, i)),
        out_shape=jax.ShapeDtypeStruct((rows, n), F32),
        compiler_params=_cparams(("parallel",)),
        name="peer_scores",
    )(x1_bf, wq_bf, k1_bf, k2_bf)


def _top_values(x, k):
    vals = []
    for _ in range(k):
        cur = jnp.max(x, axis=0, keepdims=True)
        vals.append(cur)
        x = jnp.where(x == cur, -jnp.inf, x)
    return jnp.concatenate(vals, axis=0)


def _peer_stats_kernel(s_ref, o_ref):
    half = PEER_HEADS * N_KEYS
    k = PEER_TOPK
    thetas, offs = [], []
    for h in range(PEER_HEADS):
        v1 = _top_values(s_ref[h * N_KEYS:(h + 1) * N_KEYS, :], k)
        v2 = _top_values(s_ref[half + h * N_KEYS:half + (h + 1) * N_KEYS, :], k)
        sub = lax.broadcasted_iota(jnp.int32, (8, 1), 0)
        cands = [v1[0:8] + v2[0:1], v1[8:16] + v2[0:1], v2[8:16] + v1[0:1]]
        for b in range(1, 8):
            n_a = k // (b + 1)
            cands.append(jnp.where(sub < n_a, v1[0:8] + v2[b:b + 1], -jnp.inf))
        cand = jnp.concatenate(cands, axis=0)
        work = cand
        theta = None
        for _ in range(k):
            theta = jnp.max(work, axis=0, keepdims=True)
            work = jnp.where(work == theta, -jnp.inf, work)
        m = v1[0:1] + v2[0:1]
        z = jnp.sum(jnp.where(cand >= theta, jnp.exp(cand - m), 0.0), axis=0, keepdims=True)
        thetas.append(theta)
        offs.append(-m - jnp.log(z))
    o_ref[...] = jnp.concatenate(thetas + offs, axis=0)


def _peer_stats(st, *, tl):
    rows, n = st.shape
    return pl.pallas_call(
        _peer_stats_kernel,
        grid=(n // tl,),
        in_specs=[pl.BlockSpec((rows, tl), lambda i: (0, i))],
        out_specs=pl.BlockSpec((2 * PEER_HEADS, tl), lambda i: (0, i)),
        out_shape=jax.ShapeDtypeStruct((2 * PEER_HEADS, n), F32),
        compiler_params=_cparams(("parallel",)),
        name="peer_stats",
    )(st)


def _peer_dense_kernel(xt_ref, u_ref, vt_ref, s_ref, st_ref, x_ref, g_ref, b_ref, o_ref,
                       acc_ref, p_ref, *, n_chunks, rows_per_chunk):
    j = pl.program_id(1)
    half = PEER_HEADS * N_KEYS

    @pl.when(j == 0)
    def _():
        acc_ref[...] = jnp.zeros_like(acc_ref)

    sc = jnp.dot(u_ref[...], xt_ref[...], preferred_element_type=F32)
    for jj in range(rows_per_chunk):
        i1 = j * rows_per_chunk + jj
        w = None
        for h in range(PEER_HEADS):
            s1 = s_ref[pl.ds(h * N_KEYS + i1, 1), :]
            pair = s_ref[half + h * N_KEYS:half + (h + 1) * N_KEYS, :] + s1
            gate = jnp.where(pair >= st_ref[h:h + 1, :],
                             jnp.exp(pair + st_ref[PEER_HEADS + h:PEER_HEADS + h + 1, :]), 0.0)
            w = gate if w is None else w + gate
        act = _gelu(sc[jj * N_KEYS:(jj + 1) * N_KEYS, :])
        p_ref[jj * N_KEYS:(jj + 1) * N_KEYS, :] = (w * act).astype(BF16)
    acc_ref[...] += jnp.dot(vt_ref[...], p_ref[...], preferred_element_type=F32)

    @pl.when(j == n_chunks - 1)
    def _():
        ff = acc_ref[...].T
        o_ref[...] = _layer_norm(ALPHA * x_ref[...] + ff, g_ref[...], b_ref[...])


def _peer_dense(x1t_bf, u_bf, vt_bf, st, stats, x1, g, b, *, tb, e):
    n = x1.shape[0]
    n_chunks = N_EXPERTS // e
    rows_per_chunk = e // N_KEYS
    rows = 2 * PEER_HEADS * N_KEYS
    return pl.pallas_call(
        functools.partial(_peer_dense_kernel, n_chunks=n_chunks, rows_per_chunk=rows_per_chunk),
        grid=(n // tb, n_chunks),
        in_specs=[
            pl.BlockSpec((D_MODEL, tb), lambda i, j: (0, i)),
            pl.BlockSpec((e, D_MODEL), lambda i, j: (j, 0)),
            pl.BlockSpec((D_MODEL, e), lambda i, j: (0, j)),
            pl.BlockSpec((rows, tb), lambda i, j: (0, i)),
            pl.BlockSpec((2 * PEER_HEADS, tb), lambda i, j: (0, i)),
            pl.BlockSpec((tb, D_MODEL), lambda i, j: (i, 0)),
            pl.BlockSpec((1, D_MODEL), lambda i, j: (0, 0)),
            pl.BlockSpec((1, D_MODEL), lambda i, j: (0, 0)),
        ],
        out_specs=pl.BlockSpec((tb, D_MODEL), lambda i, j: (i, 0)),
        out_shape=jax.ShapeDtypeStruct((n, D_MODEL), F32),
        scratch_shapes=[
            pltpu.VMEM((D_MODEL, tb), F32),
            pltpu.VMEM((e, tb), BF16),
        ],
        compiler_params=_cparams(("parallel", "arbitrary")),
        name="peer_dense",
    )(x1t_bf, u_bf, vt_bf, st, stats, x1, g, b)


def _rope_tables(pos):
    half = QK_DIM // 2
    inv = ROPE_THETA ** (-jnp.arange(half, dtype=F32) * 2.0 / QK_DIM)
    ang = pos.astype(F32)[:, None] * inv[None, :]
    cos, sin = jnp.cos(ang), jnp.sin(ang)
    reps = LANES // QK_DIM
    return (jnp.tile(jnp.concatenate([cos, cos], axis=1), (1, reps)),
            jnp.tile(jnp.concatenate([-sin, sin], axis=1), (1, reps)))


def _peer_mixer(x1, x1_bf, w, *, tm, tl, tb, e):
    st = _peer_scores(x1_bf, w["wq"], w["k1"], w["k2"], tm=tm)
    stats = _peer_stats(st, tl=tl)
    return _peer_dense(x1_bf.T, w["u"], w["vt"], st, stats, x1, w["ln2_g"], w["ln2_b"], tb=tb, e=e)


def kernel(x_prompt, x_sample, cache_k, cache_v, state_conv, state_h, page_table, w_in, conv_w, conv_b, lru_wa, lru_ba, lru_wx, lru_bx, lru_lambda, lambda_q1, lambda_k1, lambda_q2, lambda_k2, subln_g, w_out, ln1_g, ln1_b, peer_wq, peer_k1, peer_k2, peer_u, peer_v, ln2_g, ln2_b):
    bsz, seq, _ = x_prompt.shape
    dbsz, dseq, _ = x_sample.shape
    n_pages = page_table.shape[1]
    page = cache_k.shape[2]
    past = n_pages * page
    n = bsz * seq
    l = 0

    w_in_bf = w_in[l].astype(BF16)
    w_rec = w_in_bf[:, :2 * D_REC]
    w_q = w_in_bf[:, 2 * D_REC:2 * D_REC + D_ATT]
    w_k = w_in_bf[:, 2 * D_REC + D_ATT:2 * D_REC + 2 * D_ATT]
    w_v = w_in_bf[:, 2 * D_REC + 2 * D_ATT:]
    w_out_bf = w_out[l].astype(BF16)
    wa_bf, wx_bf = lru_wa[l].astype(BF16), lru_wx[l].astype(BF16)
    row = lambda a: a[l].reshape(1, -1)
    lam_p = jnp.stack([lambda_q1[l], lambda_k1[l], lambda_q2[l], lambda_k2[l]])
    peer_w = dict(wq=peer_wq[l].astype(BF16), k1=peer_k1[l].astype(BF16), k2=peer_k2[l].astype(BF16),
                  u=peer_u[l].astype(BF16), vt=peer_v[l].T.astype(BF16),
                  ln2_g=row(ln2_g), ln2_b=row(ln2_b))

    xp = x_prompt.reshape(n, D_MODEL)
    xp_bf = xp.astype(BF16)
    cos_p, sin_p = _rope_tables(jnp.arange(seq, dtype=jnp.int32))
    proj = functools.partial(_proj, xp_bf, cos_t=cos_p, sin_t=sin_p, tm=1024, tn=512)
    rec_p = proj(w_rec, rope=False, out_dtype=F32)
    q_p = proj(w_q, rope=True, out_dtype=BF16)
    k_p = proj(w_k, rope=True, out_dtype=F32)
    v_p = proj(w_v, rope=False, out_dtype=F32)
    rec_out_p, h_p = _rec_prompt(rec_p, conv_w[l], row(conv_b), wa_bf, wx_bf, row(lru_ba),
                                 row(lru_bx), row(lru_lambda), bsz=bsz, t=seq, cb=256)
    att_p = _attn_prompt(q_p, k_p, v_p, lam_p, row(subln_g), bsz=bsz, t=seq, tq=256)
    x1_p, x1_p_bf = _outproj(rec_out_p, att_p, w_out_bf[:D_REC], w_out_bf[D_REC:], xp,
                             row(ln1_g), row(ln1_b), tm=256)
    y_p = _peer_mixer(x1_p, x1_p_bf, peer_w, tm=256, tl=256, tb=512, e=512)

    xs = x_sample.reshape(dbsz * dseq, D_MODEL)
    ns = xs.shape[0]
    xs_bf = xs.astype(BF16)
    cos_s, sin_s = _rope_tables(jnp.full((ns,), past, dtype=jnp.int32))
    proj_s = functools.partial(_proj, xs_bf, cos_t=cos_s, sin_t=sin_s, tm=ns, tn=512)
    rec_s = proj_s(w_rec, rope=False, out_dtype=F32)
    q_s = proj_s(w_q, rope=True, out_dtype=F32)
    k_s = proj_s(w_k, rope=True, out_dtype=F32)
    v_s = proj_s(w_v, rope=False, out_dtype=F32)
    rec_out_s, conv_s_t, h_s = _rec_sample(
        rec_s, state_conv[l].transpose(1, 0, 2), state_h[l], conv_w[l], row(conv_b), wa_bf, wx_bf,
        row(lru_ba), row(lru_bx), row(lru_lambda))
    n_pool = cache_k.shape[1]
    att_s = _attn_sample(page_table, q_s.reshape(ns, 1, D_ATT), k_s.reshape(ns, 1, D_ATT),
                         v_s.reshape(ns, 1, D_ATT), cache_k[l].reshape(n_pool, page, D_ATT),
                         cache_v[l].reshape(n_pool, page, D_ATT), lam_p,
                         jnp.tile(row(subln_g), (1, N_ATT_HEADS)))
    pad = LANES - ns
    padr = lambda a: jnp.pad(a, ((0, pad), (0, 0)))
    x1_s, x1_s_bf = _outproj(padr(rec_out_s), padr(att_s.reshape(ns, D_ATT).astype(BF16)), w_out_bf[:D_REC],
                             w_out_bf[D_REC:], padr(xs), row(ln1_g), row(ln1_b), tm=LANES)
    y_s = _peer_mixer(x1_s, x1_s_bf, peer_w, tm=LANES, tl=LANES, tb=LANES, e=512)[:ns]

    k4 = (2 * N_ATT_HEADS, QK_DIM)
    v4 = (N_ATT_HEADS, ATT_HEAD_DIM)
    return (
        y_p.reshape(bsz, seq, D_MODEL),
        y_s.reshape(dbsz, dseq, D_MODEL),
        k_p.reshape(1, bsz, seq, *k4),
        v_p.reshape(1, bsz, seq, *v4),
        rec_p.reshape(bsz, seq, 2 * D_REC)[:, seq - (CONV_W - 1):, :D_REC][None],
        h_p.reshape(1, bsz, D_REC),
        k_s.reshape(1, dbsz, dseq, *k4),
        v_s.reshape(1, dbsz, dseq, *v4),
        conv_s_t.transpose(1, 0, 2)[None],
        h_s.reshape(1, dbsz, D_REC),
    )
```

```python
import functools
import math

import jax
import jax.numpy as jnp
from jax import lax
from jax.experimental import pallas as pl
from jax.experimental.pallas import tpu as pltpu

F32 = jnp.float32
BF16 = jnp.bfloat16

D_MODEL = 2048
D_REC = 1024
D_ATT = 1024
N_REC_BLOCKS = 8
REC_BLOCK = 128
CONV_W = 4
LRU_C = 8.0
N_ATT_HEADS = 8
ATT_HEAD_DIM = 128
QK_DIM = 64
ROPE_THETA = 10000.0
N_KEYS = 128
N_EXPERTS = N_KEYS * N_KEYS
PEER_HEADS = 8
PEER_TOPK = 16
LN_EPS = 1e-5
NEG_INF = -1e30
DEPTH = 1
ALPHA = (2.0 * DEPTH) ** 0.25
LAM_INIT = 0.8 - 0.6 * math.exp(-0.3 * 0)
LANES = 128

VMEM_LIMIT = 56 * 1024 * 1024


def _cparams(sem, flags=None):
    return pltpu.CompilerParams(dimension_semantics=sem, vmem_limit_bytes=VMEM_LIMIT, flags=flags)


def _gelu(x):
    c = math.sqrt(2.0 / math.pi)
    return x * (0.5 * (1.0 + jnp.tanh(c * (x + 0.044715 * (x * x * x)))))


def _layer_norm(y, g, b):
    mu = jnp.mean(y, axis=-1, keepdims=True)
    yc = y - mu
    var = jnp.mean(yc * yc, axis=-1, keepdims=True)
    return yc * lax.rsqrt(var + LN_EPS) * g + b


def _proj_kernel(x_ref, w_ref, cos_ref, sin_ref, o_ref, *, rope):
    acc = jnp.dot(x_ref[...], w_ref[...], preferred_element_type=F32)
    if rope:
        c = cos_ref[...]
        s = sin_ref[...]
        lane = lax.broadcasted_iota(jnp.int32, (1, LANES), 1)
        first_half = (lane % QK_DIM) < (QK_DIM // 2)
        outs = []
        for j in range(acc.shape[1] // LANES):
            xc = acc[:, j * LANES:(j + 1) * LANES]
            fwd = pltpu.roll(xc, LANES - QK_DIM // 2, axis=1)
            bwd = pltpu.roll(xc, QK_DIM // 2, axis=1)
            partner = jnp.where(first_half, fwd, bwd)
            outs.append(xc * c + partner * s)
        acc = jnp.concatenate(outs, axis=1)
    o_ref[...] = acc.astype(o_ref.dtype)


def _proj(x_bf, w_bf, cos_t, sin_t, *, rope, out_dtype, tm, tn):
    n, k = x_bf.shape
    m = w_bf.shape[1]
    n_pos_blocks = cos_t.shape[0] // tm
    return pl.pallas_call(
        functools.partial(_proj_kernel, rope=rope),
        grid=(n // tm, m // tn),
        in_specs=[
            pl.BlockSpec((tm, k), lambda i, j: (i, 0)),
            pl.BlockSpec((k, tn), lambda i, j: (0, j)),
            pl.BlockSpec((tm, LANES), lambda i, j: (i % n_pos_blocks, 0)),
            pl.BlockSpec((tm, LANES), lambda i, j: (i % n_pos_blocks, 0)),
        ],
        out_specs=pl.BlockSpec((tm, tn), lambda i, j: (i, j)),
        out_shape=jax.ShapeDtypeStruct((n, m), out_dtype),
        compiler_params=_cparams(("parallel", "parallel")),
        name="in_proj_rope" if rope else "in_proj",
    )(x_bf, w_bf, cos_t, sin_t)


def _lru_gates(conv, wa_ref, wx_ref, ba, bx, lam):
    rs, is_ = [], []
    for blk in range(conv.shape[1] // REC_BLOCK):
        cb = conv[:, blk * REC_BLOCK:(blk + 1) * REC_BLOCK].astype(BF16)
        rs.append(jnp.dot(cb, wa_ref[blk], preferred_element_type=F32))
        is_.append(jnp.dot(cb, wx_ref[blk], preferred_element_type=F32))
    r = jax.nn.sigmoid(jnp.concatenate(rs, axis=1) + ba)
    i = jax.nn.sigmoid(jnp.concatenate(is_, axis=1) + bx)
    softplus_neg_lam = jnp.maximum(-lam, 0.0) + jnp.log1p(jnp.exp(-jnp.abs(lam)))
    log_a = -LRU_C * r * softplus_neg_lam
    a = jnp.exp(log_a)
    b = jnp.sqrt(1.0 - jnp.exp(2.0 * log_a)) * (i * conv)
    return a, b


def _shift_rows(x, s, fill):
    row = lax.broadcasted_iota(jnp.int32, x.shape, 0)
    return jnp.where(row >= s, pltpu.roll(x, s, axis=0), fill)


def _rec_prompt_kernel(x_ref, g_ref, cw_ref, cb_ref, wa_ref, wx_ref, ba_ref, bx_ref, lam_ref,
                       o_ref, h_ref):
    x = x_ref[...]
    t = x.shape[0]
    cw = cw_ref[...]
    conv = (cb_ref[...] + cw[3:4] * x + cw[2:3] * _shift_rows(x, 1, 0.0)
            + cw[1:2] * _shift_rows(x, 2, 0.0) + cw[0:1] * _shift_rows(x, 3, 0.0))
    a, b = _lru_gates(conv, wa_ref, wx_ref, ba_ref[...], bx_ref[...], lam_ref[...])
    s = 1
    while s < t:
        b = a * _shift_rows(b, s, 0.0) + b
        a = a * _shift_rows(a, s, 1.0)
        s *= 2
    o_ref[...] = (b * _gelu(g_ref[...])).astype(o_ref.dtype)
    h_ref[0] = b[t - 1:t, :]


def _rec_prompt(rec, conv_w, conv_b, wa_bf, wx_bf, ba, bx, lam, *, bsz, t, cb):
    ncb = D_REC // cb
    return pl.pallas_call(
        _rec_prompt_kernel,
        grid=(bsz, ncb),
        in_specs=[
            pl.BlockSpec((t, cb), lambda b, c: (b, c)),
            pl.BlockSpec((t, cb), lambda b, c: (b, ncb + c)),
            pl.BlockSpec((CONV_W, cb), lambda b, c: (0, c)),
            pl.BlockSpec((1, cb), lambda b, c: (0, c)),
            pl.BlockSpec((cb // REC_BLOCK, REC_BLOCK, REC_BLOCK), lambda b, c: (c, 0, 0)),
            pl.BlockSpec((cb // REC_BLOCK, REC_BLOCK, REC_BLOCK), lambda b, c: (c, 0, 0)),
            pl.BlockSpec((1, cb), lambda b, c: (0, c)),
            pl.BlockSpec((1, cb), lambda b, c: (0, c)),
            pl.BlockSpec((1, cb), lambda b, c: (0, c)),
        ],
        out_specs=[
            pl.BlockSpec((t, cb), lambda b, c: (b, c)),
            pl.BlockSpec((1, 1, cb), lambda b, c: (b, 0, c)),
        ],
        out_shape=[
            jax.ShapeDtypeStruct((bsz * t, D_REC), BF16),
            jax.ShapeDtypeStruct((bsz, 1, D_REC), F32),
        ],
        compiler_params=_cparams(("parallel", "parallel")),
        name="rglru_prompt",
    )(rec, rec, conv_w, conv_b, wa_bf, wx_bf, ba, bx, lam)


def _rec_sample_kernel(rec_ref, sc_ref, h0_ref, cw_ref, cb_ref, wa_ref, wx_ref, ba_ref, bx_ref,
                       lam_ref, o_ref, nc_ref, h_ref):
    x = rec_ref[:, :D_REC]
    g = rec_ref[:, D_REC:]
    cw = cw_ref[...]
    conv = (cb_ref[...] + cw[0:1] * sc_ref[0] + cw[1:2] * sc_ref[1] + cw[2:3] * sc_ref[2]
            + cw[3:4] * x)
    a, b = _lru_gates(conv, wa_ref, wx_ref, ba_ref[...], bx_ref[...], lam_ref[...])
    h = a * h0_ref[...] + b
    o_ref[...] = (h * _gelu(g)).astype(o_ref.dtype)
    nc_ref[0] = sc_ref[1]
    nc_ref[1] = sc_ref[2]
    nc_ref[2] = x
    h_ref[...] = h


def _rec_sample(rec, sc_t, h0, conv_w, conv_b, wa_bf, wx_bf, ba, bx, lam):
    n = rec.shape[0]
    return pl.pallas_call(
        _rec_sample_kernel,
        out_shape=[
            jax.ShapeDtypeStruct((n, D_REC), BF16),
            jax.ShapeDtypeStruct((CONV_W - 1, n, D_REC), F32),
            jax.ShapeDtypeStruct((n, D_REC), F32),
        ],
        compiler_params=pltpu.CompilerParams(vmem_limit_bytes=VMEM_LIMIT),
        name="rglru_sample",
    )(rec, sc_t, h0, conv_w, conv_b, wa_bf, wx_bf, ba, bx, lam)


def _diff_lambda(lp):
    d1 = jnp.sum(lp[0:1] * lp[1:2], axis=1, keepdims=True)
    d2 = jnp.sum(lp[2:3] * lp[3:4], axis=1, keepdims=True)
    return jnp.exp(d1) - jnp.exp(d2) + LAM_INIT


def _nt_dot(a, b):
    return lax.dot_general(a, b, (((1,), (1,)), ((), ())), preferred_element_type=F32)


def _attn_prompt_kernel(q_ref, k_ref, v_ref, lp_ref, g_ref, o_ref, *, tq):
    qi = pl.program_id(2)
    q = q_ref[...]
    lane = lax.broadcasted_iota(jnp.int32, (1, LANES), 1)
    scale = QK_DIM ** -0.5
    qs = (q * scale).astype(BF16)
    zero = jnp.zeros_like(qs)
    q1 = jnp.where(lane < QK_DIM, qs, zero)
    q2 = jnp.where(lane >= QK_DIM, qs, zero)

    def block(kb, carry, masked):
        m1, l1, a1, m2, l2, a2 = carry
        start = pl.multiple_of(kb * tq, tq)
        k = k_ref[pl.ds(start, tq), :].astype(BF16)
        v = v_ref[pl.ds(start, tq), :].astype(BF16)

        def one(qc, m, l, acc):
            s = _nt_dot(qc, k)
            if masked:
                row = lax.broadcasted_iota(jnp.int32, s.shape, 0)
                col = lax.broadcasted_iota(jnp.int32, s.shape, 1)
                s = jnp.where(col <= row, s, NEG_INF)
            m_new = jnp.maximum(m, jnp.max(s, axis=1, keepdims=True))
            alpha = jnp.exp(m - m_new)
            p = jnp.exp(s - m_new)
            l_new = alpha * l + jnp.sum(p, axis=1, keepdims=True)
            acc_new = alpha * acc + jnp.dot(p.astype(BF16), v, preferred_element_type=F32)
            return m_new, l_new, acc_new

        m1, l1, a1 = one(q1, m1, l1, a1)
        m2, l2, a2 = one(q2, m2, l2, a2)
        return m1, l1, a1, m2, l2, a2

    m0 = jnp.full((tq, 1), NEG_INF, F32)
    l0 = jnp.zeros((tq, 1), F32)
    a0 = jnp.zeros((tq, ATT_HEAD_DIM), F32)
    carry = (m0, l0, a0, m0, l0, a0)
    carry = lax.fori_loop(0, qi, lambda kb, c: block(kb, c, False), carry)
    m1, l1, a1, m2, l2, a2 = block(qi, carry, True)
    lam = _diff_lambda(lp_ref[...])
    att = a1 / l1 - lam * (a2 / l2)
    ms = jnp.mean(att * att, axis=1, keepdims=True)
    att = att * lax.rsqrt(ms + LN_EPS) * g_ref[...] * (1.0 - LAM_INIT)
    o_ref[...] = att.astype(o_ref.dtype)


def _attn_prompt(q_bf, k, v, lam_p, subln_g, *, bsz, t, tq):
    nq = t // tq
    return pl.pallas_call(
        functools.partial(_attn_prompt_kernel, tq=tq),
        grid=(bsz, N_ATT_HEADS, nq),
        in_specs=[
            pl.BlockSpec((tq, LANES), lambda b, h, i: (b * nq + i, h)),
            pl.BlockSpec((t, LANES), lambda b, h, i: (b, h)),
            pl.BlockSpec((t, LANES), lambda b, h, i: (b, h)),
            pl.BlockSpec((4, QK_DIM), lambda b, h, i: (0, 0)),
            pl.BlockSpec((1, ATT_HEAD_DIM), lambda b, h, i: (0, 0)),
        ],
        out_specs=pl.BlockSpec((tq, LANES), lambda b, h, i: (b * nq + i, h)),
        out_shape=jax.ShapeDtypeStruct((bsz * t, D_ATT), BF16),
        compiler_params=_cparams(("parallel", "parallel", "arbitrary")),
        name="diff_attn_prompt",
    )(q_bf, k, v, lam_p, subln_g)


N_SLOTS = 2 * N_ATT_HEADS
PAGES_PER_STEP = 4


def _attn_sample_kernel(pt_ref, q_ref, kn_ref, vn_ref, *refs, n_steps, pps, page):
    del pt_ref
    k_refs, v_refs = refs[:pps], refs[pps:2 * pps]
    lp_ref, g_ref, o_ref, m_sc, l_sc, acc_sc, o_sc = refs[2 * pps:]
    step = pl.program_id(1)
    q = q_ref[0] * (QK_DIM ** -0.5)

    @pl.when(step == 0)
    def _():
        s_self = jnp.sum(q * kn_ref[0], axis=1, keepdims=True)
        m_sc[...] = jnp.broadcast_to(s_self, m_sc.shape)
        l_sc[...] = jnp.ones_like(l_sc)
        acc_sc[...] = vn_ref[0]

    lane = lax.broadcasted_iota(jnp.int32, (page, 1, LANES), 2)
    tpos = lax.broadcasted_iota(jnp.int32, (page, 1, LANES), 0)
    slot_head = lax.broadcasted_iota(jnp.int32, (N_SLOTS, LANES), 0) // 2
    for i in range(pps):
        k3 = k_refs[i][0].reshape(page, N_SLOTS, QK_DIM)
        s3 = jnp.sum(k3 * q[None], axis=2, keepdims=True)
        st = jnp.sum(jnp.where(lane == tpos, s3, 0.0), axis=0)
        m_old = m_sc[...]
        m_new = jnp.maximum(m_old, jnp.max(st, axis=1, keepdims=True))
        alpha = jnp.exp(m_old - m_new)
        p = jnp.exp(st - m_new)
        l_sc[...] = alpha * l_sc[...] + jnp.sum(p, axis=1, keepdims=True)
        m_sc[...] = m_new
        p_bf = p.astype(BF16)
        pv = jnp.zeros((N_SLOTS, ATT_HEAD_DIM), F32)
        for h in range(N_ATT_HEADS):
            v_h = v_refs[i][0, pl.ds(h, page, stride=N_ATT_HEADS), :].astype(BF16)
            full = jnp.dot(p_bf, v_h, preferred_element_type=F32)
            pv = jnp.where(slot_head == h, full, pv)
        acc_sc[...] = alpha * acc_sc[...] + pv

    @pl.when(step == n_steps - 1)
    def _():
        lam = _diff_lambda(lp_ref[...])
        o_sc[...] = acc_sc[...] / l_sc[...]
        o1 = o_sc[pl.ds(0, N_ATT_HEADS, stride=2), :]
        o2 = o_sc[pl.ds(1, N_ATT_HEADS, stride=2), :]
        att = o1 - lam * o2
        ms = jnp.mean(att * att, axis=1, keepdims=True)
        o_ref[0] = att * lax.rsqrt(ms + LN_EPS) * g_ref[...] * (1.0 - LAM_INIT)


def _attn_sample(page_table, q, k_new, v_new2, cache_k2, cache_v2, lam_p, subln_g):
    nb, n_pages = page_table.shape
    pps = PAGES_PER_STEP
    page = cache_v2.shape[1] // N_ATT_HEADS
    n_steps = n_pages // pps
    kspec = lambda i: pl.BlockSpec((1, page * N_SLOTS, QK_DIM),
                                   lambda b, s, pt: (pt[b, s * pps + i], 0, 0))
    vspec = lambda i: pl.BlockSpec((1, page * N_ATT_HEADS, ATT_HEAD_DIM),
                                   lambda b, s, pt: (pt[b, s * pps + i], 0, 0))
    grid_spec = pltpu.PrefetchScalarGridSpec(
        num_scalar_prefetch=1,
        grid=(nb, n_steps),
        in_specs=[
            pl.BlockSpec((1, N_SLOTS, QK_DIM), lambda b, s, pt: (b, 0, 0)),
            pl.BlockSpec((1, N_SLOTS, QK_DIM), lambda b, s, pt: (b, 0, 0)),
            pl.BlockSpec((1, N_SLOTS, ATT_HEAD_DIM), lambda b, s, pt: (b, 0, 0)),
            *[kspec(i) for i in range(pps)],
            *[vspec(i) for i in range(pps)],
            pl.BlockSpec((4, QK_DIM), lambda b, s, pt: (0, 0)),
            pl.BlockSpec((1, ATT_HEAD_DIM), lambda b, s, pt: (0, 0)),
        ],
        out_specs=pl.BlockSpec((1, N_ATT_HEADS, ATT_HEAD_DIM), lambda b, s, pt: (b, 0, 0)),
        scratch_shapes=[
            pltpu.VMEM((N_SLOTS, LANES), F32),
            pltpu.VMEM((N_SLOTS, LANES), F32),
            pltpu.VMEM((N_SLOTS, ATT_HEAD_DIM), F32),
            pltpu.VMEM((N_SLOTS, ATT_HEAD_DIM), F32),
        ],
    )
    return pl.pallas_call(
        functools.partial(_attn_sample_kernel, n_steps=n_steps, pps=pps, page=page),
        grid_spec=grid_spec,
        out_shape=jax.ShapeDtypeStruct((nb, N_ATT_HEADS, ATT_HEAD_DIM), F32),
        compiler_params=_cparams(("parallel", "arbitrary")),
        name="diff_attn_sample",
    )(page_table, q, k_new, v_new2, *([cache_k2] * pps), *([cache_v2] * pps), lam_p, subln_g)


def _outproj_kernel(rec_ref, att_ref, wr_ref, wa_ref, x_ref, g_ref, b_ref, o_ref, ob_ref):
    mix = jnp.dot(rec_ref[...], wr_ref[...], preferred_element_type=F32)
    mix = mix + jnp.dot(att_ref[...], wa_ref[...], preferred_element_type=F32)
    y = _layer_norm(ALPHA * x_ref[...] + mix, g_ref[...], b_ref[...])
    o_ref[...] = y
    ob_ref[...] = y.astype(BF16)


def _outproj(rec_o, att_o, w_rec, w_att, x, g, b, *, tm):
    n = x.shape[0]
    return pl.pallas_call(
        _outproj_kernel,
        grid=(n // tm,),
        in_specs=[
            pl.BlockSpec((tm, D_REC), lambda i: (i, 0)),
            pl.BlockSpec((tm, D_ATT), lambda i: (i, 0)),
            pl.BlockSpec((D_REC, D_MODEL), lambda i: (0, 0)),
            pl.BlockSpec((D_ATT, D_MODEL), lambda i: (0, 0)),
            pl.BlockSpec((tm, D_MODEL), lambda i: (i, 0)),
            pl.BlockSpec((1, D_MODEL), lambda i: (0, 0)),
            pl.BlockSpec((1, D_MODEL), lambda i: (0, 0)),
        ],
        out_specs=[
            pl.BlockSpec((tm, D_MODEL), lambda i: (i, 0)),
            pl.BlockSpec((tm, D_MODEL), lambda i: (i, 0)),
        ],
        out_shape=[
            jax.ShapeDtypeStruct((n, D_MODEL), F32),
            jax.ShapeDtypeStruct((n, D_MODEL), BF16),
        ],
        compiler_params=_cparams(("parallel",)),
        name="out_proj_ln",
    )(rec_o, att_o, w_rec, w_att, x, g, b)


def _peer_scores_kernel(x_ref, wq_ref, k1_ref, k2_ref, o_ref):
    q = jnp.dot(x_ref[...], wq_ref[...], preferred_element_type=F32).astype(BF16)
    for h in range(PEER_HEADS):
        for c, k_ref in enumerate((k1_ref, k2_ref)):
            qb = q[:, (2 * h + c) * N_KEYS:(2 * h + c + 1) * N_KEYS]
            base = (c * PEER_HEADS + h) * N_KEYS
            o_ref[base:base + N_KEYS, :] = _nt_dot(k_ref[h], qb)


def _peer_scores(x1_bf, wq_bf, k1_bf, k2_bf, *, tm):
    n = x1_bf.shape[0]
    rows = 2 * PEER_HEADS * N_KEYS
    return pl.pallas_call(
        _peer_scores_kernel,
        grid=(n // tm,),
        in_specs=[
            pl.BlockSpec((tm, D_MODEL), lambda i: (i, 0)),
            pl.BlockSpec((D_MODEL, rows), lambda i: (0, 0)),
            pl.BlockSpec((PEER_HEADS, N_KEYS, N_KEYS), lambda i: (0, 0, 0)),
            pl.BlockSpec((PEER_HEADS, N_KEYS, N_KEYS), lambda i: (0, 0, 0)),
        ],
        out_specs=pl.BlockSpec((rows, tm), lambda i: (0, i)),
        out_shape=jax.ShapeDtypeStruct((rows, n), F32),
        compiler_params=_cparams(("parallel",)),
        name="peer_scores",
    )(x1_bf, wq_bf, k1_bf, k2_bf)


def _top_values(x, k):
    vals = []
    for _ in range(k):
        cur = jnp.max(x, axis=0, keepdims=True)
        vals.append(cur)
        x = jnp.where(x == cur, -jnp.inf, x)
    return jnp.concatenate(vals, axis=0)


def _peer_stats_kernel(s_ref, e1_ref, e2_ref, t_ref):
    half = PEER_HEADS * N_KEYS
    k = PEER_TOPK
    for h in range(PEER_HEADS):
        r1 = slice(h * N_KEYS, (h + 1) * N_KEYS)
        s1 = s_ref[r1, :]
        s2 = s_ref[half + h * N_KEYS:half + (h + 1) * N_KEYS, :]
        v1 = _top_values(s1, k)
        v2 = _top_values(s2, k)
        sub = lax.broadcasted_iota(jnp.int32, (8, 1), 0)
        cands = [v1[0:8] + v2[0:1], v1[8:16] + v2[0:1], v2[8:16] + v1[0:1]]
        for b in range(1, 8):
            n_a = k // (b + 1)
            cands.append(jnp.where(sub < n_a, v1[0:8] + v2[b:b + 1], -jnp.inf))
        cand = jnp.concatenate(cands, axis=0)
        work = cand
        theta = None
        for _ in range(k):
            theta = jnp.max(work, axis=0, keepdims=True)
            work = jnp.where(work == theta, -jnp.inf, work)
        m = v1[0:1] + v2[0:1]
        z = jnp.sum(jnp.where(cand >= theta, jnp.exp(cand - m), 0.0), axis=0, keepdims=True)
        e1_ref[r1, :] = jnp.exp(s1 - (v1[0:1] + jnp.log(z)))
        e2_ref[r1, :] = jnp.exp(s2 - v2[0:1])
        thr = jnp.full(s1.shape, jnp.inf, F32)
        for b in range(k):
            thr = jnp.where(s1 + v2[b:b + 1] >= theta, v2[b:b + 1], thr)
        t_ref[r1, :] = thr


def _peer_stats(st, *, tl):
    rows, n = st.shape
    half = rows // 2
    out = jax.ShapeDtypeStruct((half, n), F32)
    return pl.pallas_call(
        _peer_stats_kernel,
        grid=(n // tl,),
        in_specs=[pl.BlockSpec((rows, tl), lambda i: (0, i))],
        out_specs=[pl.BlockSpec((half, tl), lambda i: (0, i))] * 3,
        out_shape=[out, out, out],
        compiler_params=_cparams(("parallel",)),
        name="peer_stats",
    )(st)


PEER_CHUNK = 8 * N_KEYS
PEER_SUB = 2 * N_KEYS


def _peer_dense_kernel(xt_ref, u_ref, vt_ref, s2_ref, e2_ref, e1_ref, t_ref, x_ref, g_ref, b_ref,
                       o_ref, acc_ref, sc_ref, p_ref, *, n_chunks):
    j = pl.program_id(1)
    tb = p_ref.shape[1]

    @pl.when(j == 0)
    def _():
        acc_ref[...] = jnp.zeros_like(acc_ref)

    xt = xt_ref[...]
    n_sub = PEER_CHUNK // PEER_SUB
    subs = [slice(c * PEER_SUB, (c + 1) * PEER_SUB) for c in range(n_sub)]
    sc_ref[subs[0], :] = jnp.dot(u_ref[subs[0], :], xt, preferred_element_type=F32)
    for c in range(n_sub):
        sub = subs[c]
        if c + 1 < n_sub:
            sc_ref[subs[c + 1], :] = jnp.dot(u_ref[subs[c + 1], :], xt, preferred_element_type=F32)
        for r in range(c * (PEER_SUB // N_KEYS), (c + 1) * (PEER_SUB // N_KEYS)):
            rows = slice(r * N_KEYS, (r + 1) * N_KEYS)
            for lg in range(tb // LANES):
                cols = slice(lg * LANES, (lg + 1) * LANES)
                w = None
                for h in range(PEER_HEADS):
                    hr = slice(h * N_KEYS, (h + 1) * N_KEYS)
                    gate = jnp.where(s2_ref[hr, cols] >= t_ref[h, r:r + 1, cols],
                                     e2_ref[hr, cols] * e1_ref[h, r:r + 1, cols], 0.0)
                    w = gate if w is None else w + gate
                p_ref[rows, cols] = (w * _gelu(sc_ref[rows, cols])).astype(BF16)
        acc_ref[...] += jnp.dot(vt_ref[:, sub], p_ref[sub, :], preferred_element_type=F32)

    @pl.when(j == n_chunks - 1)
    def _():
        o_ref[...] = _layer_norm(ALPHA * x_ref[...] + acc_ref[...].T, g_ref[...], b_ref[...])


def _peer_dense(x1t_bf, u_bf, vt_bf, st, e1t, e2t, tt, x1, g, b, *, tb):
    n = x1.shape[0]
    e = PEER_CHUNK
    n_chunks = N_EXPERTS // e
    rpc = e // N_KEYS
    half = PEER_HEADS * N_KEYS
    tok = lambda r: pl.BlockSpec((half, tb), lambda i, j: (r, i))
    by_head = lambda a: a.reshape(PEER_HEADS, N_KEYS, n)
    rowspec = pl.BlockSpec((PEER_HEADS, rpc, tb), lambda i, j: (0, j, i))
    return pl.pallas_call(
        functools.partial(_peer_dense_kernel, n_chunks=n_chunks),
        grid=(n // tb, n_chunks),
        in_specs=[
            pl.BlockSpec((D_MODEL, tb), lambda i, j: (0, i)),
            pl.BlockSpec((e, D_MODEL), lambda i, j: (j, 0)),
            pl.BlockSpec((D_MODEL, e), lambda i, j: (0, j)),
            tok(1),
            tok(0),
            rowspec, rowspec,
            pl.BlockSpec((tb, D_MODEL), lambda i, j: (i, 0), pipeline_mode=pl.Buffered(1)),
            pl.BlockSpec((1, D_MODEL), lambda i, j: (0, 0)),
            pl.BlockSpec((1, D_MODEL), lambda i, j: (0, 0)),
        ],
        out_specs=pl.BlockSpec((tb, D_MODEL), lambda i, j: (i, 0)),
        out_shape=jax.ShapeDtypeStruct((n, D_MODEL), F32),
        scratch_shapes=[
            pltpu.VMEM((D_MODEL, tb), F32),
            pltpu.VMEM((e, tb), F32),
            pltpu.VMEM((e, tb), BF16),
        ],
        compiler_params=_cparams(("parallel", "arbitrary")),
        name="peer_dense",
    )(x1t_bf, u_bf, vt_bf, st, e2t, by_head(e1t), by_head(tt), x1, g, b)


def _rope_tables(pos):
    half = QK_DIM // 2
    inv = ROPE_THETA ** (-jnp.arange(half, dtype=F32) * 2.0 / QK_DIM)
    ang = pos.astype(F32)[:, None] * inv[None, :]
    cos, sin = jnp.cos(ang), jnp.sin(ang)
    reps = LANES // QK_DIM
    return (jnp.tile(jnp.concatenate([cos, cos], axis=1), (1, reps)),
            jnp.tile(jnp.concatenate([-sin, sin], axis=1), (1, reps)))


def _peer_mixer(x1, x1_bf, w, *, tm, tl, tb):
    st = _peer_scores(x1_bf, w["wq"], w["k1"], w["k2"], tm=tm)
    e1t, e2t, tt = _peer_stats(st, tl=tl)
    return _peer_dense(x1_bf.T, w["u"], w["vt"], st, e1t, e2t, tt, x1, w["ln2_g"], w["ln2_b"],
                       tb=tb)


def kernel(x_prompt, x_sample, cache_k, cache_v, state_conv, state_h, page_table, w_in, conv_w, conv_b, lru_wa, lru_ba, lru_wx, lru_bx, lru_lambda, lambda_q1, lambda_k1, lambda_q2, lambda_k2, subln_g, w_out, ln1_g, ln1_b, peer_wq, peer_k1, peer_k2, peer_u, peer_v, ln2_g, ln2_b):
    bsz, seq, _ = x_prompt.shape
    dbsz, dseq, _ = x_sample.shape
    n_pages = page_table.shape[1]
    page = cache_k.shape[2]
    past = n_pages * page
    n = bsz * seq
    l = 0

    w_in_bf = w_in[l].astype(BF16)
    w_rec = w_in_bf[:, :2 * D_REC]
    w_q = w_in_bf[:, 2 * D_REC:2 * D_REC + D_ATT]
    w_k = w_in_bf[:, 2 * D_REC + D_ATT:2 * D_REC + 2 * D_ATT]
    w_v = w_in_bf[:, 2 * D_REC + 2 * D_ATT:]
    w_out_bf = w_out[l].astype(BF16)
    wa_bf, wx_bf = lru_wa[l].astype(BF16), lru_wx[l].astype(BF16)
    row = lambda a: a[l].reshape(1, -1)
    lam_p = jnp.stack([lambda_q1[l], lambda_k1[l], lambda_q2[l], lambda_k2[l]])
    peer_w = dict(wq=peer_wq[l].astype(BF16), k1=peer_k1[l].astype(BF16), k2=peer_k2[l].astype(BF16),
                  u=peer_u[l].astype(BF16), vt=peer_v[l].T.astype(BF16),
                  ln2_g=row(ln2_g), ln2_b=row(ln2_b))

    xp = x_prompt.reshape(n, D_MODEL)
    xp_bf = xp.astype(BF16)
    cos_p, sin_p = _rope_tables(jnp.arange(seq, dtype=jnp.int32))
    proj = functools.partial(_proj, xp_bf, cos_t=cos_p, sin_t=sin_p, tm=1024, tn=512)
    rec_p = proj(w_rec, rope=False, out_dtype=F32)
    q_p = proj(w_q, rope=True, out_dtype=BF16)
    k_p = proj(w_k, rope=True, out_dtype=F32)
    v_p = proj(w_v, rope=False, out_dtype=F32)
    rec_out_p, h_p = _rec_prompt(rec_p, conv_w[l], row(conv_b), wa_bf, wx_bf, row(lru_ba),
                                 row(lru_bx), row(lru_lambda), bsz=bsz, t=seq, cb=256)
    att_p = _attn_prompt(q_p, k_p, v_p, lam_p, row(subln_g), bsz=bsz, t=seq, tq=256)
    x1_p, x1_p_bf = _outproj(rec_out_p, att_p, w_out_bf[:D_REC], w_out_bf[D_REC:], xp,
                             row(ln1_g), row(ln1_b), tm=256)
    y_p = _peer_mixer(x1_p, x1_p_bf, peer_w, tm=256, tl=256, tb=512)

    xs = x_sample.reshape(dbsz * dseq, D_MODEL)
    ns = xs.shape[0]
    xs_bf = xs.astype(BF16)
    cos_s, sin_s = _rope_tables(jnp.full((ns,), past, dtype=jnp.int32))
    proj_s = functools.partial(_proj, xs_bf, cos_t=cos_s, sin_t=sin_s, tm=ns, tn=512)
    rec_s = proj_s(w_rec, rope=False, out_dtype=F32)
    q_s = proj_s(w_q, rope=True, out_dtype=F32)
    k_s = proj_s(w_k, rope=True, out_dtype=F32)
    v_s = proj_s(w_v, rope=False, out_dtype=F32)
    rec_out_s, conv_s_t, h_s = _rec_sample(
        rec_s, state_conv[l].transpose(1, 0, 2), state_h[l], conv_w[l], row(conv_b), wa_bf, wx_bf,
        row(lru_ba), row(lru_bx), row(lru_lambda))
    n_pool = cache_k.shape[1]
    att_s = _attn_sample(
        page_table, q_s.reshape(ns, N_SLOTS, QK_DIM), k_s.reshape(ns, N_SLOTS, QK_DIM),
        jnp.repeat(v_s.reshape(ns, N_ATT_HEADS, ATT_HEAD_DIM), 2, axis=1),
        cache_k[l].reshape(n_pool, page * N_SLOTS, QK_DIM),
        cache_v[l].reshape(n_pool, page * N_ATT_HEADS, ATT_HEAD_DIM), lam_p, row(subln_g))
    pad = LANES - ns
    padr = lambda a: jnp.pad(a, ((0, pad), (0, 0)))
    x1_s, x1_s_bf = _outproj(padr(rec_out_s), padr(att_s.reshape(ns, D_ATT).astype(BF16)), w_out_bf[:D_REC],
                             w_out_bf[D_REC:], padr(xs), row(ln1_g), row(ln1_b), tm=LANES)
    y_s = _peer_mixer(x1_s, x1_s_bf, peer_w, tm=LANES, tl=LANES, tb=LANES)[:ns]

    k4 = (2 * N_ATT_HEADS, QK_DIM)
    v4 = (N_ATT_HEADS, ATT_HEAD_DIM)
    return (
        y_p.reshape(bsz, seq, D_MODEL),
        y_s.reshape(dbsz, dseq, D_MODEL),
        k_p.reshape(1, bsz, seq, *k4),
        v_p.reshape(1, bsz, seq, *v4),
        rec_p.reshape(bsz, seq, 2 * D_REC)[:, seq - (CONV_W - 1):, :D_REC][None],
        h_p.reshape(1, bsz, D_REC),
        k_s.reshape(1, dbsz, dseq, *k4),
        v_s.reshape(1, dbsz, dseq, *v4),
        conv_s_t.transpose(1, 0, 2)[None],
        h_s.reshape(1, dbsz, D_REC),
    )
```

```python
import functools
import math

import jax
import jax.numpy as jnp
from jax import lax
from jax.experimental import pallas as pl
from jax.experimental.pallas import tpu as pltpu

F32 = jnp.float32
BF16 = jnp.bfloat16

D_MODEL = 2048
D_REC = 1024
D_ATT = 1024
N_REC_BLOCKS = 8
REC_BLOCK = 128
CONV_W = 4
LRU_C = 8.0
N_ATT_HEADS = 8
ATT_HEAD_DIM = 128
QK_DIM = 64
ROPE_THETA = 10000.0
N_KEYS = 128
N_EXPERTS = N_KEYS * N_KEYS
PEER_HEADS = 8
PEER_TOPK = 16
LN_EPS = 1e-5
NEG_INF = -1e30
DEPTH = 1
ALPHA = (2.0 * DEPTH) ** 0.25
LAM_INIT = 0.8 - 0.6 * math.exp(-0.3 * 0)
LANES = 128

VMEM_LIMIT = 56 * 1024 * 1024


def _cparams(sem, flags=None):
    return pltpu.CompilerParams(dimension_semantics=sem, vmem_limit_bytes=VMEM_LIMIT, flags=flags)


def _gelu(x):
    c = math.sqrt(2.0 / math.pi)
    return x * (0.5 * (1.0 + jnp.tanh(c * (x + 0.044715 * (x * x * x)))))


def _layer_norm(y, g, b):
    mu = jnp.mean(y, axis=-1, keepdims=True)
    yc = y - mu
    var = jnp.mean(yc * yc, axis=-1, keepdims=True)
    return yc * lax.rsqrt(var + LN_EPS) * g + b


def _proj_kernel(x_ref, w_ref, cos_ref, sin_ref, o_ref, *, rope):
    acc = jnp.dot(x_ref[...], w_ref[...], preferred_element_type=F32)
    if rope:
        c = cos_ref[...]
        s = sin_ref[...]
        lane = lax.broadcasted_iota(jnp.int32, (1, LANES), 1)
        first_half = (lane % QK_DIM) < (QK_DIM // 2)
        outs = []
        for j in range(acc.shape[1] // LANES):
            xc = acc[:, j * LANES:(j + 1) * LANES]
            fwd = pltpu.roll(xc, LANES - QK_DIM // 2, axis=1)
            bwd = pltpu.roll(xc, QK_DIM // 2, axis=1)
            partner = jnp.where(first_half, fwd, bwd)
            outs.append(xc * c + partner * s)
        acc = jnp.concatenate(outs, axis=1)
    o_ref[...] = acc.astype(o_ref.dtype)


def _proj(x_bf, w_bf, cos_t, sin_t, *, rope, out_dtype, tm, tn):
    n, k = x_bf.shape
    m = w_bf.shape[1]
    n_pos_blocks = cos_t.shape[0] // tm
    return pl.pallas_call(
        functools.partial(_proj_kernel, rope=rope),
        grid=(n // tm, m // tn),
        in_specs=[
            pl.BlockSpec((tm, k), lambda i, j: (i, 0)),
            pl.BlockSpec((k, tn), lambda i, j: (0, j)),
            pl.BlockSpec((tm, LANES), lambda i, j: (i % n_pos_blocks, 0)),
            pl.BlockSpec((tm, LANES), lambda i, j: (i % n_pos_blocks, 0)),
        ],
        out_specs=pl.BlockSpec((tm, tn), lambda i, j: (i, j)),
        out_shape=jax.ShapeDtypeStruct((n, m), out_dtype),
        compiler_params=_cparams(("parallel", "parallel")),
        name="in_proj_rope" if rope else "in_proj",
    )(x_bf, w_bf, cos_t, sin_t)


def _lru_gates(conv, wa_ref, wx_ref, ba, bx, lam):
    rs, is_ = [], []
    for blk in range(conv.shape[1] // REC_BLOCK):
        cb = conv[:, blk * REC_BLOCK:(blk + 1) * REC_BLOCK].astype(BF16)
        rs.append(jnp.dot(cb, wa_ref[blk], preferred_element_type=F32))
        is_.append(jnp.dot(cb, wx_ref[blk], preferred_element_type=F32))
    r = jax.nn.sigmoid(jnp.concatenate(rs, axis=1) + ba)
    i = jax.nn.sigmoid(jnp.concatenate(is_, axis=1) + bx)
    softplus_neg_lam = jnp.maximum(-lam, 0.0) + jnp.log1p(jnp.exp(-jnp.abs(lam)))
    log_a = -LRU_C * r * softplus_neg_lam
    a = jnp.exp(log_a)
    b = jnp.sqrt(1.0 - jnp.exp(2.0 * log_a)) * (i * conv)
    return a, b


def _shift_rows(x, s, fill):
    row = lax.broadcasted_iota(jnp.int32, x.shape, 0)
    return jnp.where(row >= s, pltpu.roll(x, s, axis=0), fill)


def _rec_prompt_kernel(x_ref, g_ref, cw_ref, cb_ref, wa_ref, wx_ref, ba_ref, bx_ref, lam_ref,
                       o_ref, h_ref):
    x = x_ref[...]
    t = x.shape[0]
    cw = cw_ref[...]
    conv = (cb_ref[...] + cw[3:4] * x + cw[2:3] * _shift_rows(x, 1, 0.0)
            + cw[1:2] * _shift_rows(x, 2, 0.0) + cw[0:1] * _shift_rows(x, 3, 0.0))
    a, b = _lru_gates(conv, wa_ref, wx_ref, ba_ref[...], bx_ref[...], lam_ref[...])
    s = 1
    while s < t:
        b = a * _shift_rows(b, s, 0.0) + b
        a = a * _shift_rows(a, s, 1.0)
        s *= 2
    o_ref[...] = (b * _gelu(g_ref[...])).astype(o_ref.dtype)
    h_ref[0] = b[t - 1:t, :]


def _rec_prompt(rec, conv_w, conv_b, wa_bf, wx_bf, ba, bx, lam, *, bsz, t, cb):
    ncb = D_REC // cb
    return pl.pallas_call(
        _rec_prompt_kernel,
        grid=(bsz, ncb),
        in_specs=[
            pl.BlockSpec((t, cb), lambda b, c: (b, c)),
            pl.BlockSpec((t, cb), lambda b, c: (b, ncb + c)),
            pl.BlockSpec((CONV_W, cb), lambda b, c: (0, c)),
            pl.BlockSpec((1, cb), lambda b, c: (0, c)),
            pl.BlockSpec((cb // REC_BLOCK, REC_BLOCK, REC_BLOCK), lambda b, c: (c, 0, 0)),
            pl.BlockSpec((cb // REC_BLOCK, REC_BLOCK, REC_BLOCK), lambda b, c: (c, 0, 0)),
            pl.BlockSpec((1, cb), lambda b, c: (0, c)),
            pl.BlockSpec((1, cb), lambda b, c: (0, c)),
            pl.BlockSpec((1, cb), lambda b, c: (0, c)),
        ],
        out_specs=[
            pl.BlockSpec((t, cb), lambda b, c: (b, c)),
            pl.BlockSpec((1, 1, cb), lambda b, c: (b, 0, c)),
        ],
        out_shape=[
            jax.ShapeDtypeStruct((bsz * t, D_REC), BF16),
            jax.ShapeDtypeStruct((bsz, 1, D_REC), F32),
        ],
        compiler_params=_cparams(("parallel", "parallel")),
        name="rglru_prompt",
    )(rec, rec, conv_w, conv_b, wa_bf, wx_bf, ba, bx, lam)


def _rec_sample_kernel(rec_ref, sc_ref, h0_ref, cw_ref, cb_ref, wa_ref, wx_ref, ba_ref, bx_ref,
                       lam_ref, o_ref, nc_ref, h_ref):
    x = rec_ref[:, :D_REC]
    g = rec_ref[:, D_REC:]
    cw = cw_ref[...]
    conv = (cb_ref[...] + cw[0:1] * sc_ref[0] + cw[1:2] * sc_ref[1] + cw[2:3] * sc_ref[2]
            + cw[3:4] * x)
    a, b = _lru_gates(conv, wa_ref, wx_ref, ba_ref[...], bx_ref[...], lam_ref[...])
    h = a * h0_ref[...] + b
    o_ref[...] = (h * _gelu(g)).astype(o_ref.dtype)
    nc_ref[0] = sc_ref[1]
    nc_ref[1] = sc_ref[2]
    nc_ref[2] = x
    h_ref[...] = h


def _rec_sample(rec, sc_t, h0, conv_w, conv_b, wa_bf, wx_bf, ba, bx, lam):
    n = rec.shape[0]
    return pl.pallas_call(
        _rec_sample_kernel,
        out_shape=[
            jax.ShapeDtypeStruct((n, D_REC), BF16),
            jax.ShapeDtypeStruct((CONV_W - 1, n, D_REC), F32),
            jax.ShapeDtypeStruct((n, D_REC), F32),
        ],
        compiler_params=pltpu.CompilerParams(vmem_limit_bytes=VMEM_LIMIT),
        name="rglru_sample",
    )(rec, sc_t, h0, conv_w, conv_b, wa_bf, wx_bf, ba, bx, lam)


def _diff_lambda(lp):
    d1 = jnp.sum(lp[0:1] * lp[1:2], axis=1, keepdims=True)
    d2 = jnp.sum(lp[2:3] * lp[3:4], axis=1, keepdims=True)
    return jnp.exp(d1) - jnp.exp(d2) + LAM_INIT


def _nt_dot(a, b):
    return lax.dot_general(a, b, (((1,), (1,)), ((), ())), preferred_element_type=F32)


def _attn_prompt_kernel(q_ref, k_ref, v_ref, lp_ref, g_ref, o_ref, m_sc, l_sc, acc_sc, *, tq):
    qi = pl.program_id(2)
    lane = lax.broadcasted_iota(jnp.int32, (1, LANES), 1)
    qs = q_ref[...] * (QK_DIM ** -0.5)
    zero = jnp.zeros_like(qs)
    qmaps = (jnp.where(lane < QK_DIM, qs, zero), jnp.where(lane >= QK_DIM, qs, zero))
    m_sc[...] = jnp.full(m_sc.shape, NEG_INF, F32)
    l_sc[...] = jnp.zeros_like(l_sc)
    acc_sc[...] = jnp.zeros_like(acc_sc)
    reps = tq // LANES

    def block(kb, masked):
        start = pl.multiple_of(kb * tq, tq)
        k = k_ref[pl.ds(start, tq), :].astype(BF16)
        v = v_ref[pl.ds(start, tq), :].astype(BF16)
        for c in range(2):
            s = _nt_dot(qmaps[c], k)
            if masked:
                row = lax.broadcasted_iota(jnp.int32, s.shape, 0)
                col = lax.broadcasted_iota(jnp.int32, s.shape, 1)
                s = jnp.where(col <= row, s, NEG_INF)
            m_old = m_sc[c]
            m_new = jnp.maximum(m_old, jnp.max(s, axis=1, keepdims=True))
            alpha = jnp.exp(m_old - m_new)
            p = jnp.exp(s - jnp.tile(m_new, (1, reps)))
            l_sc[c] = alpha * l_sc[c] + jnp.sum(p, axis=1, keepdims=True)
            acc_sc[c] = alpha * acc_sc[c] + jnp.dot(p.astype(BF16), v, preferred_element_type=F32)
            m_sc[c] = m_new

    @pl.loop(0, qi)
    def _(kb):
        block(kb, False)

    block(qi, True)
    lam = _diff_lambda(lp_ref[...])
    att = acc_sc[0] / l_sc[0] - lam * (acc_sc[1] / l_sc[1])
    ms = jnp.mean(att * att, axis=1, keepdims=True)
    att = att * lax.rsqrt(ms + LN_EPS) * g_ref[...] * (1.0 - LAM_INIT)
    o_ref[...] = att.astype(o_ref.dtype)


def _attn_prompt(q_bf, k, v, lam_p, subln_g, *, bsz, t, tq):
    nq = t // tq
    return pl.pallas_call(
        functools.partial(_attn_prompt_kernel, tq=tq),
        grid=(bsz, N_ATT_HEADS, nq),
        in_specs=[
            pl.BlockSpec((tq, LANES), lambda b, h, i: (b * nq + i, h)),
            pl.BlockSpec((t, LANES), lambda b, h, i: (b, h)),
            pl.BlockSpec((t, LANES), lambda b, h, i: (b, h)),
            pl.BlockSpec((4, QK_DIM), lambda b, h, i: (0, 0)),
            pl.BlockSpec((1, ATT_HEAD_DIM), lambda b, h, i: (0, 0)),
        ],
        out_specs=pl.BlockSpec((tq, LANES), lambda b, h, i: (b * nq + i, h)),
        out_shape=jax.ShapeDtypeStruct((bsz * t, D_ATT), BF16),
        scratch_shapes=[
            pltpu.VMEM((2, tq, LANES), F32),
            pltpu.VMEM((2, tq, LANES), F32),
            pltpu.VMEM((2, tq, ATT_HEAD_DIM), F32),
        ],
        compiler_params=_cparams(("parallel", "parallel", "arbitrary")),
        name="diff_attn_prompt",
    )(q_bf, k, v, lam_p, subln_g)


N_SLOTS = 2 * N_ATT_HEADS
PAGES_PER_STEP = 4


def _attn_sample_kernel(pt_ref, q_ref, kn_ref, vn_ref, *refs, n_steps, pps, page):
    del pt_ref
    k_refs, v_refs = refs[:pps], refs[pps:2 * pps]
    lp_ref, g_ref, o_ref, m_sc, l_sc, acc_sc, o_sc = refs[2 * pps:]
    step = pl.program_id(1)
    q = q_ref[0] * (QK_DIM ** -0.5)

    @pl.when(step == 0)
    def _():
        s_self = jnp.sum(q * kn_ref[0], axis=1, keepdims=True)
        m_sc[...] = jnp.broadcast_to(s_self, m_sc.shape)
        l_sc[...] = jnp.ones_like(l_sc)
        acc_sc[...] = vn_ref[0]

    lane = lax.broadcasted_iota(jnp.int32, (page, 1, LANES), 2)
    tpos = lax.broadcasted_iota(jnp.int32, (page, 1, LANES), 0)
    slot_head = lax.broadcasted_iota(jnp.int32, (N_SLOTS, LANES), 0) // 2
    for i in range(pps):
        k3 = k_refs[i][0]
        s3 = jnp.sum(k3 * q[None], axis=2, keepdims=True)
        st = jnp.sum(jnp.where(lane == tpos, s3, 0.0), axis=0)
        m_old = m_sc[...]
        m_new = jnp.maximum(m_old, jnp.max(st, axis=1, keepdims=True))
        alpha = jnp.exp(m_old - m_new)
        p = jnp.exp(st - m_new)
        l_sc[...] = alpha * l_sc[...] + jnp.sum(p, axis=1, keepdims=True)
        m_sc[...] = m_new
        p_bf = p.astype(BF16)
        pv = jnp.zeros((N_SLOTS, ATT_HEAD_DIM), F32)
        for h in range(N_ATT_HEADS):
            v_h = v_refs[i][0, pl.ds(h, page, stride=N_ATT_HEADS), :].astype(BF16)
            full = jnp.dot(p_bf, v_h, preferred_element_type=F32)
            pv = jnp.where(slot_head == h, full, pv)
        acc_sc[...] = alpha * acc_sc[...] + pv

    @pl.when(step == n_steps - 1)
    def _():
        lam = _diff_lambda(lp_ref[...])
        o_sc[...] = acc_sc[...] / l_sc[...]
        o1 = o_sc[pl.ds(0, N_ATT_HEADS, stride=2), :]
        o2 = o_sc[pl.ds(1, N_ATT_HEADS, stride=2), :]
        att = o1 - lam * o2
        ms = jnp.mean(att * att, axis=1, keepdims=True)
        o_ref[0] = att * lax.rsqrt(ms + LN_EPS) * g_ref[...] * (1.0 - LAM_INIT)


def _attn_sample(page_table, q, k_new, v_new2, cache_k2, cache_v2, lam_p, subln_g):
    nb, n_pages = page_table.shape
    pps = PAGES_PER_STEP
    page = cache_v2.shape[1] // N_ATT_HEADS
    n_steps = n_pages // pps
    kspec = lambda i: pl.BlockSpec((1, page, N_SLOTS, QK_DIM),
                                   lambda b, s, pt: (pt[b, s * pps + i], 0, 0, 0))
    vspec = lambda i: pl.BlockSpec((1, page * N_ATT_HEADS, ATT_HEAD_DIM),
                                   lambda b, s, pt: (pt[b, s * pps + i], 0, 0))
    grid_spec = pltpu.PrefetchScalarGridSpec(
        num_scalar_prefetch=1,
        grid=(nb, n_steps),
        in_specs=[
            pl.BlockSpec((1, N_SLOTS, QK_DIM), lambda b, s, pt: (b, 0, 0)),
            pl.BlockSpec((1, N_SLOTS, QK_DIM), lambda b, s, pt: (b, 0, 0)),
            pl.BlockSpec((1, N_SLOTS, ATT_HEAD_DIM), lambda b, s, pt: (b, 0, 0)),
            *[kspec(i) for i in range(pps)],
            *[vspec(i) for i in range(pps)],
            pl.BlockSpec((4, QK_DIM), lambda b, s, pt: (0, 0)),
            pl.BlockSpec((1, ATT_HEAD_DIM), lambda b, s, pt: (0, 0)),
        ],
        out_specs=pl.BlockSpec((1, N_ATT_HEADS, ATT_HEAD_DIM), lambda b, s, pt: (b, 0, 0)),
        scratch_shapes=[
            pltpu.VMEM((N_SLOTS, LANES), F32),
            pltpu.VMEM((N_SLOTS, LANES), F32),
            pltpu.VMEM((N_SLOTS, ATT_HEAD_DIM), F32),
            pltpu.VMEM((N_SLOTS, ATT_HEAD_DIM), F32),
        ],
    )
    return pl.pallas_call(
        functools.partial(_attn_sample_kernel, n_steps=n_steps, pps=pps, page=page),
        grid_spec=grid_spec,
        out_shape=jax.ShapeDtypeStruct((nb, N_ATT_HEADS, ATT_HEAD_DIM), F32),
        compiler_params=_cparams(("parallel", "arbitrary")),
        name="diff_attn_sample",
    )(page_table, q, k_new, v_new2, *([cache_k2] * pps), *([cache_v2] * pps), lam_p, subln_g)


def _outproj_kernel(rec_ref, att_ref, wr_ref, wa_ref, x_ref, g_ref, b_ref, o_ref, ob_ref):
    mix = jnp.dot(rec_ref[...], wr_ref[...], preferred_element_type=F32)
    mix = mix + jnp.dot(att_ref[...], wa_ref[...], preferred_element_type=F32)
    y = _layer_norm(ALPHA * x_ref[...] + mix, g_ref[...], b_ref[...])
    o_ref[...] = y
    ob_ref[...] = y.astype(BF16)


def _outproj(rec_o, att_o, w_rec, w_att, x, g, b, *, tm):
    n = x.shape[0]
    return pl.pallas_call(
        _outproj_kernel,
        grid=(n // tm,),
        in_specs=[
            pl.BlockSpec((tm, D_REC), lambda i: (i, 0)),
            pl.BlockSpec((tm, D_ATT), lambda i: (i, 0)),
            pl.BlockSpec((D_REC, D_MODEL), lambda i: (0, 0)),
            pl.BlockSpec((D_ATT, D_MODEL), lambda i: (0, 0)),
            pl.BlockSpec((tm, D_MODEL), lambda i: (i, 0)),
            pl.BlockSpec((1, D_MODEL), lambda i: (0, 0)),
            pl.BlockSpec((1, D_MODEL), lambda i: (0, 0)),
        ],
        out_specs=[
            pl.BlockSpec((tm, D_MODEL), lambda i: (i, 0)),
            pl.BlockSpec((tm, D_MODEL), lambda i: (i, 0)),
        ],
        out_shape=[
            jax.ShapeDtypeStruct((n, D_MODEL), F32),
            jax.ShapeDtypeStruct((n, D_MODEL), BF16),
        ],
        compiler_params=_cparams(("parallel",)),
        name="out_proj_ln",
    )(rec_o, att_o, w_rec, w_att, x, g, b)


def _peer_scores_kernel(x_ref, wq_ref, k1_ref, k2_ref, o_ref):
    q = jnp.dot(x_ref[...], wq_ref[...], preferred_element_type=F32).astype(BF16)
    for h in range(PEER_HEADS):
        for c, k_ref in enumerate((k1_ref, k2_ref)):
            qb = q[:, (2 * h + c) * N_KEYS:(2 * h + c + 1) * N_KEYS]
            base = (c * PEER_HEADS + h) * N_KEYS
            o_ref[base:base + N_KEYS, :] = _nt_dot(k_ref[h], qb)


def _peer_scores(x1_bf, wq_bf, k1_bf, k2_bf, *, tm):
    n = x1_bf.shape[0]
    rows = 2 * PEER_HEADS * N_KEYS
    return pl.pallas_call(
        _peer_scores_kernel,
        grid=(n // tm,),
        in_specs=[
            pl.BlockSpec((tm, D_MODEL), lambda i: (i, 0)),
            pl.BlockSpec((D_MODEL, rows), lambda i: (0, 0)),
            pl.BlockSpec((PEER_HEADS, N_KEYS, N_KEYS), lambda i: (0, 0, 0)),
            pl.BlockSpec((PEER_HEADS, N_KEYS, N_KEYS), lambda i: (0, 0, 0)),
        ],
        out_specs=pl.BlockSpec((rows, tm), lambda i: (0, i)),
        out_shape=jax.ShapeDtypeStruct((rows, n), F32),
        compiler_params=_cparams(("parallel",)),
        name="peer_scores",
    )(x1_bf, wq_bf, k1_bf, k2_bf)


def _top_values(x, k):
    vals = []
    for _ in range(k):
        cur = jnp.max(x, axis=0, keepdims=True)
        vals.append(cur)
        x = jnp.where(x == cur, -jnp.inf, x)
    return jnp.concatenate(vals, axis=0)


def _peer_stats_kernel(s_ref, e1_ref, e2_ref, t_ref):
    half = PEER_HEADS * N_KEYS
    k = PEER_TOPK
    for h in range(PEER_HEADS):
        r1 = slice(h * N_KEYS, (h + 1) * N_KEYS)
        s1 = s_ref[r1, :]
        s2 = s_ref[half + h * N_KEYS:half + (h + 1) * N_KEYS, :]
        v1 = _top_values(s1, k)
        v2 = _top_values(s2, k)
        sub = lax.broadcasted_iota(jnp.int32, (8, 1), 0)
        cands = [v1[0:8] + v2[0:1], v1[8:16] + v2[0:1], v2[8:16] + v1[0:1]]
        for b in range(1, 8):
            n_a = k // (b + 1)
            cands.append(jnp.where(sub < n_a, v1[0:8] + v2[b:b + 1], -jnp.inf))
        cand = jnp.concatenate(cands, axis=0)
        work = cand
        theta = None
        for _ in range(k):
            theta = jnp.max(work, axis=0, keepdims=True)
            work = jnp.where(work == theta, -jnp.inf, work)
        m = v1[0:1] + v2[0:1]
        z = jnp.sum(jnp.where(cand >= theta, jnp.exp(cand - m), 0.0), axis=0, keepdims=True)
        e1_ref[r1, :] = jnp.exp(s1 - (v1[0:1] + jnp.log(z)))
        e2_ref[r1, :] = jnp.exp(s2 - v2[0:1])
        thr = jnp.full(s1.shape, jnp.inf, F32)
        for b in range(k):
            thr = jnp.where(s1 + v2[b:b + 1] >= theta, v2[b:b + 1], thr)
        t_ref[r1, :] = thr


def _peer_stats(st, *, tl):
    rows, n = st.shape
    half = rows // 2
    out = jax.ShapeDtypeStruct((half, n), F32)
    return pl.pallas_call(
        _peer_stats_kernel,
        grid=(n // tl,),
        in_specs=[pl.BlockSpec((rows, tl), lambda i: (0, i))],
        out_specs=[pl.BlockSpec((half, tl), lambda i: (0, i))] * 3,
        out_shape=[out, out, out],
        compiler_params=_cparams(("parallel",)),
        name="peer_stats",
    )(st)


PEER_CHUNK = 8 * N_KEYS
PEER_SUB = 2 * N_KEYS


def _peer_dense_kernel(xt_ref, u_ref, vt_ref, s2_ref, e2_ref, e1_ref, t_ref, x_ref, g_ref, b_ref,
                       o_ref, acc_ref, sc_ref, p_ref, *, n_chunks):
    j = pl.program_id(1)
    tb = p_ref.shape[1]

    @pl.when(j == 0)
    def _():
        acc_ref[...] = jnp.zeros_like(acc_ref)

    xt = xt_ref[...]
    n_sub = PEER_CHUNK // PEER_SUB
    subs = [slice(c * PEER_SUB, (c + 1) * PEER_SUB) for c in range(n_sub)]
    sc_ref[subs[0], :] = jnp.dot(u_ref[subs[0], :], xt, preferred_element_type=F32)
    for c in range(n_sub):
        sub = subs[c]
        if c + 1 < n_sub:
            sc_ref[subs[c + 1], :] = jnp.dot(u_ref[subs[c + 1], :], xt, preferred_element_type=F32)
        for r in range(c * (PEER_SUB // N_KEYS), (c + 1) * (PEER_SUB // N_KEYS)):
            rows = slice(r * N_KEYS, (r + 1) * N_KEYS)
            for lg in range(tb // LANES):
                cols = slice(lg * LANES, (lg + 1) * LANES)
                w = None
                for h in range(PEER_HEADS):
                    hr = slice(h * N_KEYS, (h + 1) * N_KEYS)
                    thr = jnp.broadcast_to(t_ref[h, r:r + 1, cols], (8, LANES))[None]
                    e1 = jnp.broadcast_to(e1_ref[h, r:r + 1, cols], (8, LANES))[None]
                    s2 = s2_ref[hr, cols].reshape(N_KEYS // 8, 8, LANES)
                    e2 = e2_ref[hr, cols].reshape(N_KEYS // 8, 8, LANES)
                    gate = jnp.where(s2 >= thr, e2 * e1, 0.0)
                    w = gate if w is None else w + gate
                act = _gelu(sc_ref[rows, cols])
                p_ref[rows, cols] = (w.reshape(N_KEYS, LANES) * act).astype(BF16)
        acc_ref[...] += jnp.dot(vt_ref[:, sub], p_ref[sub, :], preferred_element_type=F32)

    @pl.when(j == n_chunks - 1)
    def _():
        o_ref[...] = _layer_norm(ALPHA * x_ref[...] + acc_ref[...].T, g_ref[...], b_ref[...])


def _peer_dense(x1t_bf, u_bf, vt_bf, st, e1t, e2t, tt, x1, g, b, *, tb):
    n = x1.shape[0]
    e = PEER_CHUNK
    n_chunks = N_EXPERTS // e
    rpc = e // N_KEYS
    half = PEER_HEADS * N_KEYS
    tok = lambda r: pl.BlockSpec((half, tb), lambda i, j: (r, i))
    by_head = lambda a: a.reshape(PEER_HEADS, N_KEYS, n)
    rowspec = pl.BlockSpec((PEER_HEADS, rpc, tb), lambda i, j: (0, j, i))
    return pl.pallas_call(
        functools.partial(_peer_dense_kernel, n_chunks=n_chunks),
        grid=(n // tb, n_chunks),
        in_specs=[
            pl.BlockSpec((D_MODEL, tb), lambda i, j: (0, i)),
            pl.BlockSpec((e, D_MODEL), lambda i, j: (j, 0)),
            pl.BlockSpec((D_MODEL, e), lambda i, j: (0, j)),
            tok(1),
            tok(0),
            rowspec, rowspec,
            pl.BlockSpec((tb, D_MODEL), lambda i, j: (i, 0), pipeline_mode=pl.Buffered(1)),
            pl.BlockSpec((1, D_MODEL), lambda i, j: (0, 0)),
            pl.BlockSpec((1, D_MODEL), lambda i, j: (0, 0)),
        ],
        out_specs=pl.BlockSpec((tb, D_MODEL), lambda i, j: (i, 0)),
        out_shape=jax.ShapeDtypeStruct((n, D_MODEL), F32),
        scratch_shapes=[
            pltpu.VMEM((D_MODEL, tb), F32),
            pltpu.VMEM((e, tb), F32),
            pltpu.VMEM((e, tb), BF16),
        ],
        compiler_params=_cparams(("parallel", "arbitrary")),
        name="peer_dense",
    )(x1t_bf, u_bf, vt_bf, st, e2t, by_head(e1t), by_head(tt), x1, g, b)


def _rope_tables(pos):
    half = QK_DIM // 2
    inv = ROPE_THETA ** (-jnp.arange(half, dtype=F32) * 2.0 / QK_DIM)
    ang = pos.astype(F32)[:, None] * inv[None, :]
    cos, sin = jnp.cos(ang), jnp.sin(ang)
    reps = LANES // QK_DIM
    return (jnp.tile(jnp.concatenate([cos, cos], axis=1), (1, reps)),
            jnp.tile(jnp.concatenate([-sin, sin], axis=1), (1, reps)))


def _peer_mixer(x1, x1_bf, w, *, tm, tl, tb):
    st = _peer_scores(x1_bf, w["wq"], w["k1"], w["k2"], tm=tm)
    e1t, e2t, tt = _peer_stats(st, tl=tl)
    return _peer_dense(x1_bf.T, w["u"], w["vt"], st, e1t, e2t, tt, x1, w["ln2_g"], w["ln2_b"],
                       tb=tb)


def kernel(x_prompt, x_sample, cache_k, cache_v, state_conv, state_h, page_table, w_in, conv_w, conv_b, lru_wa, lru_ba, lru_wx, lru_bx, lru_lambda, lambda_q1, lambda_k1, lambda_q2, lambda_k2, subln_g, w_out, ln1_g, ln1_b, peer_wq, peer_k1, peer_k2, peer_u, peer_v, ln2_g, ln2_b):
    bsz, seq, _ = x_prompt.shape
    dbsz, dseq, _ = x_sample.shape
    n_pages = page_table.shape[1]
    page = cache_k.shape[2]
    past = n_pages * page
    n = bsz * seq
    l = 0

    w_in_bf = w_in[l].astype(BF16)
    w_rec = w_in_bf[:, :2 * D_REC]
    w_q = w_in_bf[:, 2 * D_REC:2 * D_REC + D_ATT]
    w_k = w_in_bf[:, 2 * D_REC + D_ATT:2 * D_REC + 2 * D_ATT]
    w_v = w_in_bf[:, 2 * D_REC + 2 * D_ATT:]
    w_out_bf = w_out[l].astype(BF16)
    wa_bf, wx_bf = lru_wa[l].astype(BF16), lru_wx[l].astype(BF16)
    row = lambda a: a[l].reshape(1, -1)
    lam_p = jnp.stack([lambda_q1[l], lambda_k1[l], lambda_q2[l], lambda_k2[l]])
    peer_w = dict(wq=peer_wq[l].astype(BF16), k1=peer_k1[l].astype(BF16), k2=peer_k2[l].astype(BF16),
                  u=peer_u[l].astype(BF16), vt=peer_v[l].T.astype(BF16),
                  ln2_g=row(ln2_g), ln2_b=row(ln2_b))

    xp = x_prompt.reshape(n, D_MODEL)
    xp_bf = xp.astype(BF16)
    cos_p, sin_p = _rope_tables(jnp.arange(seq, dtype=jnp.int32))
    proj = functools.partial(_proj, xp_bf, cos_t=cos_p, sin_t=sin_p, tm=1024, tn=512)
    rec_p = proj(w_rec, rope=False, out_dtype=F32)
    q_p = proj(w_q, rope=True, out_dtype=BF16)
    k_p = proj(w_k, rope=True, out_dtype=F32)
    v_p = proj(w_v, rope=False, out_dtype=F32)
    rec_out_p, h_p = _rec_prompt(rec_p, conv_w[l], row(conv_b), wa_bf, wx_bf, row(lru_ba),
                                 row(lru_bx), row(lru_lambda), bsz=bsz, t=seq, cb=256)
    att_p = _attn_prompt(q_p, k_p, v_p, lam_p, row(subln_g), bsz=bsz, t=seq, tq=512)
    x1_p, x1_p_bf = _outproj(rec_out_p, att_p, w_out_bf[:D_REC], w_out_bf[D_REC:], xp,
                             row(ln1_g), row(ln1_b), tm=256)
    y_p = _peer_mixer(x1_p, x1_p_bf, peer_w, tm=256, tl=256, tb=512)

    xs = x_sample.reshape(dbsz * dseq, D_MODEL)
    ns = xs.shape[0]
    xs_bf = xs.astype(BF16)
    cos_s, sin_s = _rope_tables(jnp.full((ns,), past, dtype=jnp.int32))
    proj_s = functools.partial(_proj, xs_bf, cos_t=cos_s, sin_t=sin_s, tm=ns, tn=512)
    rec_s = proj_s(w_rec, rope=False, out_dtype=F32)
    q_s = proj_s(w_q, rope=True, out_dtype=F32)
    k_s = proj_s(w_k, rope=True, out_dtype=F32)
    v_s = proj_s(w_v, rope=False, out_dtype=F32)
    rec_out_s, conv_s_t, h_s = _rec_sample(
        rec_s, state_conv[l].transpose(1, 0, 2), state_h[l], conv_w[l], row(conv_b), wa_bf, wx_bf,
        row(lru_ba), row(lru_bx), row(lru_lambda))
    n_pool = cache_k.shape[1]
    att_s = _attn_sample(
        page_table, q_s.reshape(ns, N_SLOTS, QK_DIM), k_s.reshape(ns, N_SLOTS, QK_DIM),
        jnp.repeat(v_s.reshape(ns, N_ATT_HEADS, ATT_HEAD_DIM), 2, axis=1),
        cache_k[l],
        cache_v[l].reshape(n_pool, page * N_ATT_HEADS, ATT_HEAD_DIM), lam_p, row(subln_g))
    pad = LANES - ns
    padr = lambda a: jnp.pad(a, ((0, pad), (0, 0)))
    x1_s, x1_s_bf = _outproj(padr(rec_out_s), padr(att_s.reshape(ns, D_ATT).astype(BF16)), w_out_bf[:D_REC],
                             w_out_bf[D_REC:], padr(xs), row(ln1_g), row(ln1_b), tm=LANES)
    y_s = _peer_mixer(x1_s, x1_s_bf, peer_w, tm=LANES, tl=LANES, tb=LANES)[:ns]

    k4 = (2 * N_ATT_HEADS, QK_DIM)
    v4 = (N_ATT_HEADS, ATT_HEAD_DIM)
    return (
        y_p.reshape(bsz, seq, D_MODEL),
        y_s.reshape(dbsz, dseq, D_MODEL),
        k_p.reshape(1, bsz, seq, *k4),
        v_p.reshape(1, bsz, seq, *v4),
        rec_p.reshape(bsz, seq, 2 * D_REC)[:, seq - (CONV_W - 1):, :D_REC][None],
        h_p.reshape(1, bsz, D_REC),
        k_s.reshape(1, dbsz, dseq, *k4),
        v_s.reshape(1, dbsz, dseq, *v4),
        conv_s_t.transpose(1, 0, 2)[None],
        h_s.reshape(1, dbsz, D_REC),
    )
```

```python
import functools
import itertools
import math

import jax
import jax.numpy as jnp
from jax import lax
from jax.experimental import pallas as pl
from jax.experimental.pallas import tpu as pltpu

F32 = jnp.float32
BF16 = jnp.bfloat16

D_MODEL = 2048
D_REC = 1024
D_ATT = 1024
N_REC_BLOCKS = 8
REC_BLOCK = 128
CONV_W = 4
LRU_C = 8.0
N_ATT_HEADS = 8
ATT_HEAD_DIM = 128
QK_DIM = 64
ROPE_THETA = 10000.0
N_KEYS = 128
N_EXPERTS = N_KEYS * N_KEYS
PEER_HEADS = 8
PEER_TOPK = 16
LN_EPS = 1e-5
NEG_INF = -1e30
DEPTH = 1
ALPHA = (2.0 * DEPTH) ** 0.25
LAM_INIT = 0.8 - 0.6 * math.exp(-0.3 * 0)
LANES = 128

VMEM_LIMIT = 56 * 1024 * 1024


def _cparams(sem, flags=None):
    return pltpu.CompilerParams(dimension_semantics=sem, vmem_limit_bytes=VMEM_LIMIT, flags=flags)


def _gelu(x):
    c = math.sqrt(2.0 / math.pi)
    return x * (0.5 * (1.0 + jnp.tanh(c * (x + 0.044715 * (x * x * x)))))


def _layer_norm(y, g, b):
    mu = jnp.mean(y, axis=-1, keepdims=True)
    yc = y - mu
    var = jnp.mean(yc * yc, axis=-1, keepdims=True)
    return yc * lax.rsqrt(var + LN_EPS) * g + b


def _proj_kernel(x_ref, w_ref, cos_ref, sin_ref, o_ref, *, rope):
    acc = jnp.dot(x_ref[...], w_ref[...], preferred_element_type=F32)
    if rope:
        c = cos_ref[...]
        s = sin_ref[...]
        lane = lax.broadcasted_iota(jnp.int32, (1, LANES), 1)
        first_half = (lane % QK_DIM) < (QK_DIM // 2)
        outs = []
        for j in range(acc.shape[1] // LANES):
            xc = acc[:, j * LANES:(j + 1) * LANES]
            fwd = pltpu.roll(xc, LANES - QK_DIM // 2, axis=1)
            bwd = pltpu.roll(xc, QK_DIM // 2, axis=1)
            partner = jnp.where(first_half, fwd, bwd)
            outs.append(xc * c + partner * s)
        acc = jnp.concatenate(outs, axis=1)
    o_ref[...] = acc.astype(o_ref.dtype)


def _proj(x_bf, w_bf, cos_t, sin_t, *, rope, out_dtype, tm, tn):
    n, k = x_bf.shape
    m = w_bf.shape[1]
    n_pos_blocks = cos_t.shape[0] // tm
    return pl.pallas_call(
        functools.partial(_proj_kernel, rope=rope),
        grid=(n // tm, m // tn),
        in_specs=[
            pl.BlockSpec((tm, k), lambda i, j: (i, 0)),
            pl.BlockSpec((k, tn), lambda i, j: (0, j)),
            pl.BlockSpec((tm, LANES), lambda i, j: (i % n_pos_blocks, 0)),
            pl.BlockSpec((tm, LANES), lambda i, j: (i % n_pos_blocks, 0)),
        ],
        out_specs=pl.BlockSpec((tm, tn), lambda i, j: (i, j)),
        out_shape=jax.ShapeDtypeStruct((n, m), out_dtype),
        compiler_params=_cparams(("parallel", "parallel")),
        name="in_proj_rope" if rope else "in_proj",
    )(x_bf, w_bf, cos_t, sin_t)


def _lru_gates(conv, wa_ref, wx_ref, ba, bx, lam):
    rs, is_ = [], []
    for blk in range(conv.shape[1] // REC_BLOCK):
        cb = conv[:, blk * REC_BLOCK:(blk + 1) * REC_BLOCK].astype(BF16)
        rs.append(jnp.dot(cb, wa_ref[blk], preferred_element_type=F32))
        is_.append(jnp.dot(cb, wx_ref[blk], preferred_element_type=F32))
    r = jax.nn.sigmoid(jnp.concatenate(rs, axis=1) + ba)
    i = jax.nn.sigmoid(jnp.concatenate(is_, axis=1) + bx)
    softplus_neg_lam = jnp.maximum(-lam, 0.0) + jnp.log1p(jnp.exp(-jnp.abs(lam)))
    log_a = -LRU_C * r * softplus_neg_lam
    a = jnp.exp(log_a)
    b = jnp.sqrt(1.0 - jnp.exp(2.0 * log_a)) * (i * conv)
    return a, b


def _shift_rows(x, s, fill):
    row = lax.broadcasted_iota(jnp.int32, x.shape, 0)
    return jnp.where(row >= s, pltpu.roll(x, s, axis=0), fill)


def _rec_prompt_kernel(x_ref, g_ref, cw_ref, cb_ref, wa_ref, wx_ref, ba_ref, bx_ref, lam_ref,
                       o_ref, h_ref):
    x = x_ref[...]
    t = x.shape[0]
    cw = cw_ref[...]
    conv = (cb_ref[...] + cw[3:4] * x + cw[2:3] * _shift_rows(x, 1, 0.0)
            + cw[1:2] * _shift_rows(x, 2, 0.0) + cw[0:1] * _shift_rows(x, 3, 0.0))
    a, b = _lru_gates(conv, wa_ref, wx_ref, ba_ref[...], bx_ref[...], lam_ref[...])
    s = 1
    while s < t:
        b = a * _shift_rows(b, s, 0.0) + b
        a = a * _shift_rows(a, s, 1.0)
        s *= 2
    o_ref[...] = (b * _gelu(g_ref[...])).astype(o_ref.dtype)
    h_ref[0] = b[t - 1:t, :]


def _rec_prompt(rec, conv_w, conv_b, wa_bf, wx_bf, ba, bx, lam, *, bsz, t, cb):
    ncb = D_REC // cb
    return pl.pallas_call(
        _rec_prompt_kernel,
        grid=(bsz, ncb),
        in_specs=[
            pl.BlockSpec((t, cb), lambda b, c: (b, c)),
            pl.BlockSpec((t, cb), lambda b, c: (b, ncb + c)),
            pl.BlockSpec((CONV_W, cb), lambda b, c: (0, c)),
            pl.BlockSpec((1, cb), lambda b, c: (0, c)),
            pl.BlockSpec((cb // REC_BLOCK, REC_BLOCK, REC_BLOCK), lambda b, c: (c, 0, 0)),
            pl.BlockSpec((cb // REC_BLOCK, REC_BLOCK, REC_BLOCK), lambda b, c: (c, 0, 0)),
            pl.BlockSpec((1, cb), lambda b, c: (0, c)),
            pl.BlockSpec((1, cb), lambda b, c: (0, c)),
            pl.BlockSpec((1, cb), lambda b, c: (0, c)),
        ],
        out_specs=[
            pl.BlockSpec((t, cb), lambda b, c: (b, c)),
            pl.BlockSpec((1, 1, cb), lambda b, c: (b, 0, c)),
        ],
        out_shape=[
            jax.ShapeDtypeStruct((bsz * t, D_REC), BF16),
            jax.ShapeDtypeStruct((bsz, 1, D_REC), F32),
        ],
        compiler_params=_cparams(("parallel", "parallel")),
        name="rglru_prompt",
    )(rec, rec, conv_w, conv_b, wa_bf, wx_bf, ba, bx, lam)


def _rec_sample_kernel(rec_ref, sc_ref, h0_ref, cw_ref, cb_ref, wa_ref, wx_ref, ba_ref, bx_ref,
                       lam_ref, o_ref, nc_ref, h_ref):
    x = rec_ref[:, :D_REC]
    g = rec_ref[:, D_REC:]
    cw = cw_ref[...]
    conv = (cb_ref[...] + cw[0:1] * sc_ref[0] + cw[1:2] * sc_ref[1] + cw[2:3] * sc_ref[2]
            + cw[3:4] * x)
    a, b = _lru_gates(conv, wa_ref, wx_ref, ba_ref[...], bx_ref[...], lam_ref[...])
    h = a * h0_ref[...] + b
    o_ref[...] = (h * _gelu(g)).astype(o_ref.dtype)
    nc_ref[0] = sc_ref[1]
    nc_ref[1] = sc_ref[2]
    nc_ref[2] = x
    h_ref[...] = h


def _rec_sample(rec, sc_t, h0, conv_w, conv_b, wa_bf, wx_bf, ba, bx, lam):
    n = rec.shape[0]
    return pl.pallas_call(
        _rec_sample_kernel,
        out_shape=[
            jax.ShapeDtypeStruct((n, D_REC), BF16),
            jax.ShapeDtypeStruct((CONV_W - 1, n, D_REC), F32),
            jax.ShapeDtypeStruct((n, D_REC), F32),
        ],
        compiler_params=pltpu.CompilerParams(vmem_limit_bytes=VMEM_LIMIT),
        name="rglru_sample",
    )(rec, sc_t, h0, conv_w, conv_b, wa_bf, wx_bf, ba, bx, lam)


def _diff_lambda(lp):
    d1 = jnp.sum(lp[0:1] * lp[1:2], axis=1, keepdims=True)
    d2 = jnp.sum(lp[2:3] * lp[3:4], axis=1, keepdims=True)
    return jnp.exp(d1) - jnp.exp(d2) + LAM_INIT


def _nt_dot(a, b):
    return lax.dot_general(a, b, (((1,), (1,)), ((), ())), preferred_element_type=F32)


def _attn_prompt_kernel(q_ref, k_ref, v_ref, lp_ref, g_ref, o_ref, m_sc, l_sc, acc_sc, *, tq):
    qi = pl.program_id(2)
    lane = lax.broadcasted_iota(jnp.int32, (1, LANES), 1)
    qs = q_ref[...] * (QK_DIM ** -0.5)
    zero = jnp.zeros_like(qs)
    qmaps = (jnp.where(lane < QK_DIM, qs, zero), jnp.where(lane >= QK_DIM, qs, zero))
    m_sc[...] = jnp.full(m_sc.shape, NEG_INF, F32)
    l_sc[...] = jnp.zeros_like(l_sc)
    acc_sc[...] = jnp.zeros_like(acc_sc)
    reps = tq // LANES

    def block(kb, masked):
        start = pl.multiple_of(kb * tq, tq)
        k = k_ref[pl.ds(start, tq), :].astype(BF16)
        v = v_ref[pl.ds(start, tq), :].astype(BF16)
        for c in range(2):
            s = _nt_dot(qmaps[c], k)
            if masked:
                row = lax.broadcasted_iota(jnp.int32, s.shape, 0)
                col = lax.broadcasted_iota(jnp.int32, s.shape, 1)
                s = jnp.where(col <= row, s, NEG_INF)
            m_old = m_sc[c]
            m_new = jnp.maximum(m_old, jnp.max(s, axis=1, keepdims=True))
            alpha = jnp.exp(m_old - m_new)
            p = jnp.exp(s - jnp.tile(m_new, (1, reps)))
            l_sc[c] = alpha * l_sc[c] + jnp.sum(p, axis=1, keepdims=True)
            acc_sc[c] = alpha * acc_sc[c] + jnp.dot(p.astype(BF16), v, preferred_element_type=F32)
            m_sc[c] = m_new

    @pl.loop(0, qi)
    def _(kb):
        block(kb, False)

    block(qi, True)
    lam = _diff_lambda(lp_ref[...])
    att = acc_sc[0] / l_sc[0] - lam * (acc_sc[1] / l_sc[1])
    ms = jnp.mean(att * att, axis=1, keepdims=True)
    att = att * lax.rsqrt(ms + LN_EPS) * g_ref[...] * (1.0 - LAM_INIT)
    o_ref[...] = att.astype(o_ref.dtype)


def _attn_prompt(q_bf, k, v, lam_p, subln_g, *, bsz, t, tq):
    nq = t // tq
    return pl.pallas_call(
        functools.partial(_attn_prompt_kernel, tq=tq),
        grid=(bsz, N_ATT_HEADS, nq),
        in_specs=[
            pl.BlockSpec((tq, LANES), lambda b, h, i: (b * nq + i, h)),
            pl.BlockSpec((t, LANES), lambda b, h, i: (b, h)),
            pl.BlockSpec((t, LANES), lambda b, h, i: (b, h)),
            pl.BlockSpec((4, QK_DIM), lambda b, h, i: (0, 0)),
            pl.BlockSpec((1, ATT_HEAD_DIM), lambda b, h, i: (0, 0)),
        ],
        out_specs=pl.BlockSpec((tq, LANES), lambda b, h, i: (b * nq + i, h)),
        out_shape=jax.ShapeDtypeStruct((bsz * t, D_ATT), BF16),
        scratch_shapes=[
            pltpu.VMEM((2, tq, LANES), F32),
            pltpu.VMEM((2, tq, LANES), F32),
            pltpu.VMEM((2, tq, ATT_HEAD_DIM), F32),
        ],
        compiler_params=_cparams(("parallel", "parallel", "arbitrary")),
        name="diff_attn_prompt",
    )(q_bf, k, v, lam_p, subln_g)


N_SLOTS = 2 * N_ATT_HEADS
PAGES_PER_STEP = 4


def _attn_sample_kernel(pt_ref, q_ref, qb_ref, kn_ref, vn_ref, *refs, n_steps, pps, page):
    del pt_ref
    k_refs, v_refs = refs[:pps], refs[pps:2 * pps]
    lp_ref, g_ref, o_ref, m_sc, l_sc, acc_sc, o_sc = refs[2 * pps:]
    step = pl.program_id(1)
    q = q_ref[0] * (QK_DIM ** -0.5)

    @pl.when(step == 0)
    def _():
        s_self = jnp.sum(q * kn_ref[0], axis=1, keepdims=True)
        m_sc[...] = jnp.broadcast_to(s_self, m_sc.shape)
        l_sc[...] = jnp.ones_like(l_sc)
        acc_sc[...] = vn_ref[0]

    qb = qb_ref[0]
    slot_head = lax.broadcasted_iota(jnp.int32, (N_SLOTS, LANES), 0) // 2
    for i in range(pps):
        st = jnp.sum(k_refs[i][0] * qb, axis=1) * (QK_DIM ** -0.5)
        m_old = m_sc[...]
        m_new = jnp.maximum(m_old, jnp.max(st, axis=1, keepdims=True))
        alpha = jnp.exp(m_old - m_new)
        p = jnp.exp(st - m_new)
        l_sc[...] = alpha * l_sc[...] + jnp.sum(p, axis=1, keepdims=True)
        m_sc[...] = m_new
        p_bf = p.astype(BF16)
        pv = jnp.zeros((N_SLOTS, ATT_HEAD_DIM), F32)
        for h in range(N_ATT_HEADS):
            v_h = v_refs[i][0, pl.ds(h, page, stride=N_ATT_HEADS), :].astype(BF16)
            full = jnp.dot(p_bf, v_h, preferred_element_type=F32)
            pv = jnp.where(slot_head == h, full, pv)
        acc_sc[...] = alpha * acc_sc[...] + pv

    @pl.when(step == n_steps - 1)
    def _():
        lam = _diff_lambda(lp_ref[...])
        o_sc[...] = acc_sc[...] / l_sc[...]
        o1 = o_sc[pl.ds(0, N_ATT_HEADS, stride=2), :]
        o2 = o_sc[pl.ds(1, N_ATT_HEADS, stride=2), :]
        att = o1 - lam * o2
        ms = jnp.mean(att * att, axis=1, keepdims=True)
        o_ref[0] = att * lax.rsqrt(ms + LN_EPS) * g_ref[...] * (1.0 - LAM_INIT)


def _attn_sample(page_table, q, k_new, v_new2, cache_kt, cache_v2, lam_p, subln_g):
    nb, n_pages = page_table.shape
    pps = PAGES_PER_STEP
    page = cache_kt.shape[3]
    qb = jnp.broadcast_to(q[..., None], (*q.shape, page))
    n_steps = n_pages // pps
    kspec = lambda i: pl.BlockSpec((1, N_SLOTS, QK_DIM, page),
                                   lambda b, s, pt: (pt[b, s * pps + i], 0, 0, 0))
    vspec = lambda i: pl.BlockSpec((1, page * N_ATT_HEADS, ATT_HEAD_DIM),
                                   lambda b, s, pt: (pt[b, s * pps + i], 0, 0))
    grid_spec = pltpu.PrefetchScalarGridSpec(
        num_scalar_prefetch=1,
        grid=(nb, n_steps),
        in_specs=[
            pl.BlockSpec((1, N_SLOTS, QK_DIM), lambda b, s, pt: (b, 0, 0)),
            pl.BlockSpec((1, N_SLOTS, QK_DIM, page), lambda b, s, pt: (b, 0, 0, 0)),
            pl.BlockSpec((1, N_SLOTS, QK_DIM), lambda b, s, pt: (b, 0, 0)),
            pl.BlockSpec((1, N_SLOTS, ATT_HEAD_DIM), lambda b, s, pt: (b, 0, 0)),
            *[kspec(i) for i in range(pps)],
            *[vspec(i) for i in range(pps)],
            pl.BlockSpec((4, QK_DIM), lambda b, s, pt: (0, 0)),
            pl.BlockSpec((1, ATT_HEAD_DIM), lambda b, s, pt: (0, 0)),
        ],
        out_specs=pl.BlockSpec((1, N_ATT_HEADS, ATT_HEAD_DIM), lambda b, s, pt: (b, 0, 0)),
        scratch_shapes=[
            pltpu.VMEM((N_SLOTS, LANES), F32),
            pltpu.VMEM((N_SLOTS, LANES), F32),
            pltpu.VMEM((N_SLOTS, ATT_HEAD_DIM), F32),
            pltpu.VMEM((N_SLOTS, ATT_HEAD_DIM), F32),
        ],
    )
    return pl.pallas_call(
        functools.partial(_attn_sample_kernel, n_steps=n_steps, pps=pps, page=page),
        grid_spec=grid_spec,
        out_shape=jax.ShapeDtypeStruct((nb, N_ATT_HEADS, ATT_HEAD_DIM), F32),
        compiler_params=_cparams(("parallel", "arbitrary")),
        name="diff_attn_sample",
    )(page_table, q, qb, k_new, v_new2, *([cache_kt] * pps), *([cache_v2] * pps), lam_p, subln_g)


def _outproj_kernel(rec_ref, att_ref, wr_ref, wa_ref, x_ref, g_ref, b_ref, o_ref, ob_ref):
    mix = jnp.dot(rec_ref[...], wr_ref[...], preferred_element_type=F32)
    mix = mix + jnp.dot(att_ref[...], wa_ref[...], preferred_element_type=F32)
    y = _layer_norm(ALPHA * x_ref[...] + mix, g_ref[...], b_ref[...])
    o_ref[...] = y
    ob_ref[...] = y.astype(BF16)


def _outproj(rec_o, att_o, w_rec, w_att, x, g, b, *, tm):
    n = x.shape[0]
    return pl.pallas_call(
        _outproj_kernel,
        grid=(n // tm,),
        in_specs=[
            pl.BlockSpec((tm, D_REC), lambda i: (i, 0)),
            pl.BlockSpec((tm, D_ATT), lambda i: (i, 0)),
            pl.BlockSpec((D_REC, D_MODEL), lambda i: (0, 0)),
            pl.BlockSpec((D_ATT, D_MODEL), lambda i: (0, 0)),
            pl.BlockSpec((tm, D_MODEL), lambda i: (i, 0)),
            pl.BlockSpec((1, D_MODEL), lambda i: (0, 0)),
            pl.BlockSpec((1, D_MODEL), lambda i: (0, 0)),
        ],
        out_specs=[
            pl.BlockSpec((tm, D_MODEL), lambda i: (i, 0)),
            pl.BlockSpec((tm, D_MODEL), lambda i: (i, 0)),
        ],
        out_shape=[
            jax.ShapeDtypeStruct((n, D_MODEL), F32),
            jax.ShapeDtypeStruct((n, D_MODEL), BF16),
        ],
        compiler_params=_cparams(("parallel",)),
        name="out_proj_ln",
    )(rec_o, att_o, w_rec, w_att, x, g, b)


def _peer_scores_kernel(x_ref, wq_ref, k1_ref, k2_ref, o_ref):
    q = jnp.dot(x_ref[...], wq_ref[...], preferred_element_type=F32).astype(BF16)
    for h in range(PEER_HEADS):
        for c, k_ref in enumerate((k1_ref, k2_ref)):
            qb = q[:, (2 * h + c) * N_KEYS:(2 * h + c + 1) * N_KEYS]
            base = (c * PEER_HEADS + h) * N_KEYS
            o_ref[base:base + N_KEYS, :] = _nt_dot(k_ref[h], qb)


def _peer_scores(x1_bf, wq_bf, k1_bf, k2_bf, *, tm):
    n = x1_bf.shape[0]
    rows = 2 * PEER_HEADS * N_KEYS
    return pl.pallas_call(
        _peer_scores_kernel,
        grid=(n // tm,),
        in_specs=[
            pl.BlockSpec((tm, D_MODEL), lambda i: (i, 0)),
            pl.BlockSpec((D_MODEL, rows), lambda i: (0, 0)),
            pl.BlockSpec((PEER_HEADS, N_KEYS, N_KEYS), lambda i: (0, 0, 0)),
            pl.BlockSpec((PEER_HEADS, N_KEYS, N_KEYS), lambda i: (0, 0, 0)),
        ],
        out_specs=pl.BlockSpec((rows, tm), lambda i: (0, i)),
        out_shape=jax.ShapeDtypeStruct((rows, n), F32),
        compiler_params=_cparams(("parallel",)),
        name="peer_scores",
    )(x1_bf, wq_bf, k1_bf, k2_bf)


def _top_values(x, k):
    vals = []
    for _ in range(k):
        cur = jnp.max(x, axis=0, keepdims=True)
        vals.append(cur)
        x = jnp.where(x == cur, -jnp.inf, x)
    return jnp.concatenate(vals, axis=0)


def _peer_stats_kernel(s_ref, e1_ref, e2_ref, t_ref):
    half = PEER_HEADS * N_KEYS
    k = PEER_TOPK
    for h in range(PEER_HEADS):
        r1 = slice(h * N_KEYS, (h + 1) * N_KEYS)
        s1 = s_ref[r1, :]
        s2 = s_ref[half + h * N_KEYS:half + (h + 1) * N_KEYS, :]
        v1 = _top_values(s1, k)
        v2 = _top_values(s2, k)
        sub = lax.broadcasted_iota(jnp.int32, (8, 1), 0)
        cands = [v1[0:8] + v2[0:1], v1[8:16] + v2[0:1], v2[8:16] + v1[0:1]]
        for b in range(1, 8):
            n_a = k // (b + 1)
            cands.append(jnp.where(sub < n_a, v1[0:8] + v2[b:b + 1], -jnp.inf))
        cand = jnp.concatenate(cands, axis=0)
        work = cand
        theta = None
        for _ in range(k):
            theta = jnp.max(work, axis=0, keepdims=True)
            work = jnp.where(work == theta, -jnp.inf, work)
        m = v1[0:1] + v2[0:1]
        z = jnp.sum(jnp.where(cand >= theta, jnp.exp(cand - m), 0.0), axis=0, keepdims=True)
        e1_ref[r1, :] = jnp.exp(s1 - (v1[0:1] + jnp.log(z)))
        e2_ref[r1, :] = jnp.exp(s2 - v2[0:1])
        thr = jnp.full(s1.shape, jnp.inf, F32)
        for b in range(k):
            thr = jnp.where(s1 + v2[b:b + 1] >= theta, v2[b:b + 1], thr)
        t_ref[r1, :] = thr


def _peer_stats(st, *, tl):
    rows, n = st.shape
    half = rows // 2
    out = jax.ShapeDtypeStruct((half, n), F32)
    return pl.pallas_call(
        _peer_stats_kernel,
        grid=(n // tl,),
        in_specs=[pl.BlockSpec((rows, tl), lambda i: (0, i))],
        out_specs=[pl.BlockSpec((half, tl), lambda i: (0, i))] * 3,
        out_shape=[out, out, out],
        compiler_params=_cparams(("parallel",)),
        name="peer_stats",
    )(st)


PEER_CHUNK = 8 * N_KEYS
PEER_SUB = 2 * N_KEYS
GATE_ROWS = 64


def _peer_dense_kernel(xt_ref, u_ref, vt_ref, s2_ref, e2_ref, e1_ref, t_ref, x_ref, g_ref, b_ref,
                       o_ref, acc_ref, sc_ref, p_ref, *, n_chunks):
    j = pl.program_id(1)
    tb = p_ref.shape[1]

    @pl.when(j == 0)
    def _():
        acc_ref[...] = jnp.zeros_like(acc_ref)

    xt = xt_ref[...]
    n_sub = PEER_CHUNK // PEER_SUB
    subs = [slice(c * PEER_SUB, (c + 1) * PEER_SUB) for c in range(n_sub)]
    sc_ref[subs[0], :] = jnp.dot(u_ref[subs[0], :], xt, preferred_element_type=F32)
    for c in range(n_sub):
        sub = subs[c]
        if c + 1 < n_sub:
            sc_ref[subs[c + 1], :] = jnp.dot(u_ref[subs[c + 1], :], xt, preferred_element_type=F32)
        for r in range(c * (PEER_SUB // N_KEYS), (c + 1) * (PEER_SUB // N_KEYS)):
            for i2h, lg in itertools.product(range(N_KEYS // GATE_ROWS), range(tb // LANES)):
                cols = slice(lg * LANES, (lg + 1) * LANES)
                rows = slice(r * N_KEYS + i2h * GATE_ROWS, r * N_KEYS + (i2h + 1) * GATE_ROWS)
                w = None
                for h in range(PEER_HEADS):
                    hr = slice(h * N_KEYS + i2h * GATE_ROWS, h * N_KEYS + (i2h + 1) * GATE_ROWS)
                    gate = jnp.where(s2_ref[hr, cols] >= t_ref[h, r:r + 1, cols],
                                     e2_ref[hr, cols] * e1_ref[h, r:r + 1, cols], 0.0)
                    w = gate if w is None else w + gate
                p_ref[rows, cols] = (w * _gelu(sc_ref[rows, cols])).astype(BF16)
        acc_ref[...] += jnp.dot(vt_ref[:, sub], p_ref[sub, :], preferred_element_type=F32)

    @pl.when(j == n_chunks - 1)
    def _():
        o_ref[...] = _layer_norm(ALPHA * x_ref[...] + acc_ref[...].T, g_ref[...], b_ref[...])


def _peer_dense(x1t_bf, u_bf, vt_bf, st, e1t, e2t, tt, x1, g, b, *, tb):
    n = x1.shape[0]
    e = PEER_CHUNK
    n_chunks = N_EXPERTS // e
    rpc = e // N_KEYS
    half = PEER_HEADS * N_KEYS
    tok = lambda r: pl.BlockSpec((half, tb), lambda i, j: (r, i))
    by_head = lambda a: a.reshape(PEER_HEADS, N_KEYS, n)
    rowspec = pl.BlockSpec((PEER_HEADS, rpc, tb), lambda i, j: (0, j, i))
    return pl.pallas_call(
        functools.partial(_peer_dense_kernel, n_chunks=n_chunks),
        grid=(n // tb, n_chunks),
        in_specs=[
            pl.BlockSpec((D_MODEL, tb), lambda i, j: (0, i)),
            pl.BlockSpec((e, D_MODEL), lambda i, j: (j, 0)),
            pl.BlockSpec((D_MODEL, e), lambda i, j: (0, j)),
            tok(1),
            tok(0),
            rowspec, rowspec,
            pl.BlockSpec((tb, D_MODEL), lambda i, j: (i, 0), pipeline_mode=pl.Buffered(1)),
            pl.BlockSpec((1, D_MODEL), lambda i, j: (0, 0)),
            pl.BlockSpec((1, D_MODEL), lambda i, j: (0, 0)),
        ],
        out_specs=pl.BlockSpec((tb, D_MODEL), lambda i, j: (i, 0)),
        out_shape=jax.ShapeDtypeStruct((n, D_MODEL), F32),
        scratch_shapes=[
            pltpu.VMEM((D_MODEL, tb), F32),
            pltpu.VMEM((e, tb), F32),
            pltpu.VMEM((e, tb), BF16),
        ],
        compiler_params=_cparams(("parallel", "arbitrary")),
        name="peer_dense",
    )(x1t_bf, u_bf, vt_bf, st, e2t, by_head(e1t), by_head(tt), x1, g, b)


def _rope_tables(pos):
    half = QK_DIM // 2
    inv = ROPE_THETA ** (-jnp.arange(half, dtype=F32) * 2.0 / QK_DIM)
    ang = pos.astype(F32)[:, None] * inv[None, :]
    cos, sin = jnp.cos(ang), jnp.sin(ang)
    reps = LANES // QK_DIM
    return (jnp.tile(jnp.concatenate([cos, cos], axis=1), (1, reps)),
            jnp.tile(jnp.concatenate([-sin, sin], axis=1), (1, reps)))


def _peer_mixer(x1, x1_bf, w, *, tm, tl, tb):
    st = _peer_scores(x1_bf, w["wq"], w["k1"], w["k2"], tm=tm)
    e1t, e2t, tt = _peer_stats(st, tl=tl)
    return _peer_dense(x1_bf.T, w["u"], w["vt"], st, e1t, e2t, tt, x1, w["ln2_g"], w["ln2_b"],
                       tb=tb)


def kernel(x_prompt, x_sample, cache_k, cache_v, state_conv, state_h, page_table, w_in, conv_w, conv_b, lru_wa, lru_ba, lru_wx, lru_bx, lru_lambda, lambda_q1, lambda_k1, lambda_q2, lambda_k2, subln_g, w_out, ln1_g, ln1_b, peer_wq, peer_k1, peer_k2, peer_u, peer_v, ln2_g, ln2_b):
    bsz, seq, _ = x_prompt.shape
    dbsz, dseq, _ = x_sample.shape
    n_pages = page_table.shape[1]
    page = cache_k.shape[2]
    past = n_pages * page
    n = bsz * seq
    l = 0

    w_in_bf = w_in[l].astype(BF16)
    w_rec = w_in_bf[:, :2 * D_REC]
    w_q = w_in_bf[:, 2 * D_REC:2 * D_REC + D_ATT]
    w_k = w_in_bf[:, 2 * D_REC + D_ATT:2 * D_REC + 2 * D_ATT]
    w_v = w_in_bf[:, 2 * D_REC + 2 * D_ATT:]
    w_out_bf = w_out[l].astype(BF16)
    wa_bf, wx_bf = lru_wa[l].astype(BF16), lru_wx[l].astype(BF16)
    row = lambda a: a[l].reshape(1, -1)
    lam_p = jnp.stack([lambda_q1[l], lambda_k1[l], lambda_q2[l], lambda_k2[l]])
    peer_w = dict(wq=peer_wq[l].astype(BF16), k1=peer_k1[l].astype(BF16), k2=peer_k2[l].astype(BF16),
                  u=peer_u[l].astype(BF16), vt=peer_v[l].T.astype(BF16),
                  ln2_g=row(ln2_g), ln2_b=row(ln2_b))

    xp = x_prompt.reshape(n, D_MODEL)
    xp_bf = xp.astype(BF16)
    cos_p, sin_p = _rope_tables(jnp.arange(seq, dtype=jnp.int32))
    proj = functools.partial(_proj, xp_bf, cos_t=cos_p, sin_t=sin_p, tm=1024, tn=512)
    rec_p = proj(w_rec, rope=False, out_dtype=F32)
    q_p = proj(w_q, rope=True, out_dtype=BF16)
    k_p = proj(w_k, rope=True, out_dtype=F32)
    v_p = proj(w_v, rope=False, out_dtype=F32)
    rec_out_p, h_p = _rec_prompt(rec_p, conv_w[l], row(conv_b), wa_bf, wx_bf, row(lru_ba),
                                 row(lru_bx), row(lru_lambda), bsz=bsz, t=seq, cb=256)
    att_p = _attn_prompt(q_p, k_p, v_p, lam_p, row(subln_g), bsz=bsz, t=seq, tq=512)
    x1_p, x1_p_bf = _outproj(rec_out_p, att_p, w_out_bf[:D_REC], w_out_bf[D_REC:], xp,
                             row(ln1_g), row(ln1_b), tm=256)
    y_p = _peer_mixer(x1_p, x1_p_bf, peer_w, tm=256, tl=256, tb=512)

    xs = x_sample.reshape(dbsz * dseq, D_MODEL)
    ns = xs.shape[0]
    xs_bf = xs.astype(BF16)
    cos_s, sin_s = _rope_tables(jnp.full((ns,), past, dtype=jnp.int32))
    proj_s = functools.partial(_proj, xs_bf, cos_t=cos_s, sin_t=sin_s, tm=ns, tn=512)
    rec_s = proj_s(w_rec, rope=False, out_dtype=F32)
    q_s = proj_s(w_q, rope=True, out_dtype=F32)
    k_s = proj_s(w_k, rope=True, out_dtype=F32)
    v_s = proj_s(w_v, rope=False, out_dtype=F32)
    rec_out_s, conv_s_t, h_s = _rec_sample(
        rec_s, state_conv[l].transpose(1, 0, 2), state_h[l], conv_w[l], row(conv_b), wa_bf, wx_bf,
        row(lru_ba), row(lru_bx), row(lru_lambda))
    n_pool = cache_k.shape[1]
    att_s = _attn_sample(
        page_table, q_s.reshape(ns, N_SLOTS, QK_DIM), k_s.reshape(ns, N_SLOTS, QK_DIM),
        jnp.repeat(v_s.reshape(ns, N_ATT_HEADS, ATT_HEAD_DIM), 2, axis=1),
        cache_k[l].transpose(0, 2, 3, 1),
        cache_v[l].reshape(n_pool, page * N_ATT_HEADS, ATT_HEAD_DIM), lam_p, row(subln_g))
    pad = LANES - ns
    padr = lambda a: jnp.pad(a, ((0, pad), (0, 0)))
    x1_s, x1_s_bf = _outproj(padr(rec_out_s), padr(att_s.reshape(ns, D_ATT).astype(BF16)), w_out_bf[:D_REC],
                             w_out_bf[D_REC:], padr(xs), row(ln1_g), row(ln1_b), tm=LANES)
    y_s = _peer_mixer(x1_s, x1_s_bf, peer_w, tm=LANES, tl=LANES, tb=LANES)[:ns]

    k4 = (2 * N_ATT_HEADS, QK_DIM)
    v4 = (N_ATT_HEADS, ATT_HEAD_DIM)
    return (
        y_p.reshape(bsz, seq, D_MODEL),
        y_s.reshape(dbsz, dseq, D_MODEL),
        k_p.reshape(1, bsz, seq, *k4),
        v_p.reshape(1, bsz, seq, *v4),
        rec_p.reshape(bsz, seq, 2 * D_REC)[:, seq - (CONV_W - 1):, :D_REC][None],
        h_p.reshape(1, bsz, D_REC),
        k_s.reshape(1, dbsz, dseq, *k4),
        v_s.reshape(1, dbsz, dseq, *v4),
        conv_s_t.transpose(1, 0, 2)[None],
        h_s.reshape(1, dbsz, D_REC),
    )
```

```python
import functools
import itertools
import math

import jax
import jax.numpy as jnp
from jax import lax
from jax.experimental import pallas as pl
from jax.experimental.pallas import tpu as pltpu

F32 = jnp.float32
BF16 = jnp.bfloat16

D_MODEL = 2048
D_REC = 1024
D_ATT = 1024
N_REC_BLOCKS = 8
REC_BLOCK = 128
CONV_W = 4
LRU_C = 8.0
N_ATT_HEADS = 8
ATT_HEAD_DIM = 128
QK_DIM = 64
ROPE_THETA = 10000.0
N_KEYS = 128
N_EXPERTS = N_KEYS * N_KEYS
PEER_HEADS = 8
PEER_TOPK = 16
LN_EPS = 1e-5
NEG_INF = -1e30
DEPTH = 1
ALPHA = (2.0 * DEPTH) ** 0.25
LAM_INIT = 0.8 - 0.6 * math.exp(-0.3 * 0)
LANES = 128

VMEM_LIMIT = 56 * 1024 * 1024


def _cparams(sem, flags=None):
    return pltpu.CompilerParams(dimension_semantics=sem, vmem_limit_bytes=VMEM_LIMIT, flags=flags)


def _gelu(x):
    c = math.sqrt(2.0 / math.pi)
    return x * (0.5 * (1.0 + jnp.tanh(c * (x + 0.044715 * (x * x * x)))))


def _layer_norm(y, g, b):
    mu = jnp.mean(y, axis=-1, keepdims=True)
    yc = y - mu
    var = jnp.mean(yc * yc, axis=-1, keepdims=True)
    return yc * lax.rsqrt(var + LN_EPS) * g + b


def _proj_kernel(x_ref, w_ref, cos_ref, sin_ref, o_ref, *, rope):
    acc = jnp.dot(x_ref[...], w_ref[...], preferred_element_type=F32)
    if rope:
        c = cos_ref[...]
        s = sin_ref[...]
        lane = lax.broadcasted_iota(jnp.int32, (1, LANES), 1)
        first_half = (lane % QK_DIM) < (QK_DIM // 2)
        outs = []
        for j in range(acc.shape[1] // LANES):
            xc = acc[:, j * LANES:(j + 1) * LANES]
            fwd = pltpu.roll(xc, LANES - QK_DIM // 2, axis=1)
            bwd = pltpu.roll(xc, QK_DIM // 2, axis=1)
            partner = jnp.where(first_half, fwd, bwd)
            outs.append(xc * c + partner * s)
        acc = jnp.concatenate(outs, axis=1)
    o_ref[...] = acc.astype(o_ref.dtype)


def _proj(x_bf, w_bf, cos_t, sin_t, *, rope, out_dtype, tm, tn):
    n, k = x_bf.shape
    m = w_bf.shape[1]
    n_pos_blocks = cos_t.shape[0] // tm
    return pl.pallas_call(
        functools.partial(_proj_kernel, rope=rope),
        grid=(n // tm, m // tn),
        in_specs=[
            pl.BlockSpec((tm, k), lambda i, j: (i, 0)),
            pl.BlockSpec((k, tn), lambda i, j: (0, j)),
            pl.BlockSpec((tm, LANES), lambda i, j: (i % n_pos_blocks, 0)),
            pl.BlockSpec((tm, LANES), lambda i, j: (i % n_pos_blocks, 0)),
        ],
        out_specs=pl.BlockSpec((tm, tn), lambda i, j: (i, j)),
        out_shape=jax.ShapeDtypeStruct((n, m), out_dtype),
        compiler_params=_cparams(("parallel", "parallel")),
        name="in_proj_rope" if rope else "in_proj",
    )(x_bf, w_bf, cos_t, sin_t)


def _lru_gates(conv, wa_ref, wx_ref, ba, bx, lam):
    rs, is_ = [], []
    for blk in range(conv.shape[1] // REC_BLOCK):
        cb = conv[:, blk * REC_BLOCK:(blk + 1) * REC_BLOCK].astype(BF16)
        rs.append(jnp.dot(cb, wa_ref[blk], preferred_element_type=F32))
        is_.append(jnp.dot(cb, wx_ref[blk], preferred_element_type=F32))
    r = jax.nn.sigmoid(jnp.concatenate(rs, axis=1) + ba)
    i = jax.nn.sigmoid(jnp.concatenate(is_, axis=1) + bx)
    softplus_neg_lam = jnp.maximum(-lam, 0.0) + jnp.log1p(jnp.exp(-jnp.abs(lam)))
    log_a = -LRU_C * r * softplus_neg_lam
    a = jnp.exp(log_a)
    b = jnp.sqrt(1.0 - jnp.exp(2.0 * log_a)) * (i * conv)
    return a, b


def _shift_rows(x, s, fill):
    row = lax.broadcasted_iota(jnp.int32, x.shape, 0)
    return jnp.where(row >= s, pltpu.roll(x, s, axis=0), fill)


def _rec_prompt_kernel(x_ref, g_ref, cw_ref, cb_ref, wa_ref, wx_ref, ba_ref, bx_ref, lam_ref,
                       o_ref, h_ref):
    x = x_ref[...]
    t = x.shape[0]
    cw = cw_ref[...]
    conv = (cb_ref[...] + cw[3:4] * x + cw[2:3] * _shift_rows(x, 1, 0.0)
            + cw[1:2] * _shift_rows(x, 2, 0.0) + cw[0:1] * _shift_rows(x, 3, 0.0))
    a, b = _lru_gates(conv, wa_ref, wx_ref, ba_ref[...], bx_ref[...], lam_ref[...])
    s = 1
    while s < t:
        b = a * _shift_rows(b, s, 0.0) + b
        a = a * _shift_rows(a, s, 1.0)
        s *= 2
    o_ref[...] = (b * _gelu(g_ref[...])).astype(o_ref.dtype)
    h_ref[0] = b[t - 1:t, :]


def _rec_prompt(rec, conv_w, conv_b, wa_bf, wx_bf, ba, bx, lam, *, bsz, t, cb):
    ncb = D_REC // cb
    return pl.pallas_call(
        _rec_prompt_kernel,
        grid=(bsz, ncb),
        in_specs=[
            pl.BlockSpec((t, cb), lambda b, c: (b, c)),
            pl.BlockSpec((t, cb), lambda b, c: (b, ncb + c)),
            pl.BlockSpec((CONV_W, cb), lambda b, c: (0, c)),
            pl.BlockSpec((1, cb), lambda b, c: (0, c)),
            pl.BlockSpec((cb // REC_BLOCK, REC_BLOCK, REC_BLOCK), lambda b, c: (c, 0, 0)),
            pl.BlockSpec((cb // REC_BLOCK, REC_BLOCK, REC_BLOCK), lambda b, c: (c, 0, 0)),
            pl.BlockSpec((1, cb), lambda b, c: (0, c)),
            pl.BlockSpec((1, cb), lambda b, c: (0, c)),
            pl.BlockSpec((1, cb), lambda b, c: (0, c)),
        ],
        out_specs=[
            pl.BlockSpec((t, cb), lambda b, c: (b, c)),
            pl.BlockSpec((1, 1, cb), lambda b, c: (b, 0, c)),
        ],
        out_shape=[
            jax.ShapeDtypeStruct((bsz * t, D_REC), BF16),
            jax.ShapeDtypeStruct((bsz, 1, D_REC), F32),
        ],
        compiler_params=_cparams(("parallel", "parallel")),
        name="rglru_prompt",
    )(rec, rec, conv_w, conv_b, wa_bf, wx_bf, ba, bx, lam)


def _rec_sample_kernel(rec_ref, sc_ref, h0_ref, cw_ref, cb_ref, wa_ref, wx_ref, ba_ref, bx_ref,
                       lam_ref, o_ref, nc_ref, h_ref):
    x = rec_ref[:, :D_REC]
    g = rec_ref[:, D_REC:]
    cw = cw_ref[...]
    conv = (cb_ref[...] + cw[0:1] * sc_ref[0] + cw[1:2] * sc_ref[1] + cw[2:3] * sc_ref[2]
            + cw[3:4] * x)
    a, b = _lru_gates(conv, wa_ref, wx_ref, ba_ref[...], bx_ref[...], lam_ref[...])
    h = a * h0_ref[...] + b
    o_ref[...] = (h * _gelu(g)).astype(o_ref.dtype)
    nc_ref[0] = sc_ref[1]
    nc_ref[1] = sc_ref[2]
    nc_ref[2] = x
    h_ref[...] = h


def _rec_sample(rec, sc_t, h0, conv_w, conv_b, wa_bf, wx_bf, ba, bx, lam):
    n = rec.shape[0]
    return pl.pallas_call(
        _rec_sample_kernel,
        out_shape=[
            jax.ShapeDtypeStruct((n, D_REC), BF16),
            jax.ShapeDtypeStruct((CONV_W - 1, n, D_REC), F32),
            jax.ShapeDtypeStruct((n, D_REC), F32),
        ],
        compiler_params=pltpu.CompilerParams(vmem_limit_bytes=VMEM_LIMIT),
        name="rglru_sample",
    )(rec, sc_t, h0, conv_w, conv_b, wa_bf, wx_bf, ba, bx, lam)


def _diff_lambda(lp):
    d1 = jnp.sum(lp[0:1] * lp[1:2], axis=1, keepdims=True)
    d2 = jnp.sum(lp[2:3] * lp[3:4], axis=1, keepdims=True)
    return jnp.exp(d1) - jnp.exp(d2) + LAM_INIT


def _nt_dot(a, b):
    return lax.dot_general(a, b, (((1,), (1,)), ((), ())), preferred_element_type=F32)


def _attn_prompt_kernel(q_ref, k_ref, v_ref, lp_ref, g_ref, o_ref, m_sc, l_sc, acc_sc, *, tq):
    qi = pl.program_id(2)
    lane = lax.broadcasted_iota(jnp.int32, (1, LANES), 1)
    qs = q_ref[...] * (QK_DIM ** -0.5)
    zero = jnp.zeros_like(qs)
    qmaps = (jnp.where(lane < QK_DIM, qs, zero), jnp.where(lane >= QK_DIM, qs, zero))
    m_sc[...] = jnp.full(m_sc.shape, NEG_INF, F32)
    l_sc[...] = jnp.zeros_like(l_sc)
    acc_sc[...] = jnp.zeros_like(acc_sc)
    reps = tq // LANES

    def block(kb, masked):
        start = pl.multiple_of(kb * tq, tq)
        k = k_ref[pl.ds(start, tq), :].astype(BF16)
        v = v_ref[pl.ds(start, tq), :].astype(BF16)
        for c in range(2):
            s = _nt_dot(qmaps[c], k)
            if masked:
                row = lax.broadcasted_iota(jnp.int32, s.shape, 0)
                col = lax.broadcasted_iota(jnp.int32, s.shape, 1)
                s = jnp.where(col <= row, s, NEG_INF)
            m_old = m_sc[c]
            m_new = jnp.maximum(m_old, jnp.max(s, axis=1, keepdims=True))
            alpha = jnp.exp(m_old - m_new)
            p = jnp.exp(s - jnp.tile(m_new, (1, reps)))
            l_sc[c] = alpha * l_sc[c] + jnp.sum(p, axis=1, keepdims=True)
            acc_sc[c] = alpha * acc_sc[c] + jnp.dot(p.astype(BF16), v, preferred_element_type=F32)
            m_sc[c] = m_new

    @pl.loop(0, qi)
    def _(kb):
        block(kb, False)

    block(qi, True)
    lam = _diff_lambda(lp_ref[...])
    att = acc_sc[0] / l_sc[0] - lam * (acc_sc[1] / l_sc[1])
    ms = jnp.mean(att * att, axis=1, keepdims=True)
    att = att * lax.rsqrt(ms + LN_EPS) * g_ref[...] * (1.0 - LAM_INIT)
    o_ref[...] = att.astype(o_ref.dtype)


def _attn_prompt(q_bf, k, v, lam_p, subln_g, *, bsz, t, tq):
    nq = t // tq
    return pl.pallas_call(
        functools.partial(_attn_prompt_kernel, tq=tq),
        grid=(bsz, N_ATT_HEADS, nq),
        in_specs=[
            pl.BlockSpec((tq, LANES), lambda b, h, i: (b * nq + i, h)),
            pl.BlockSpec((t, LANES), lambda b, h, i: (b, h)),
            pl.BlockSpec((t, LANES), lambda b, h, i: (b, h)),
            pl.BlockSpec((4, QK_DIM), lambda b, h, i: (0, 0)),
            pl.BlockSpec((1, ATT_HEAD_DIM), lambda b, h, i: (0, 0)),
        ],
        out_specs=pl.BlockSpec((tq, LANES), lambda b, h, i: (b * nq + i, h)),
        out_shape=jax.ShapeDtypeStruct((bsz * t, D_ATT), BF16),
        scratch_shapes=[
            pltpu.VMEM((2, tq, LANES), F32),
            pltpu.VMEM((2, tq, LANES), F32),
            pltpu.VMEM((2, tq, ATT_HEAD_DIM), F32),
        ],
        compiler_params=_cparams(("parallel", "parallel", "arbitrary")),
        name="diff_attn_prompt",
    )(q_bf, k, v, lam_p, subln_g)


N_SLOTS = 2 * N_ATT_HEADS
PAGES_PER_STEP = 4


def _attn_sample_kernel(pt_ref, q_ref, qb_ref, kn_ref, vn_ref, *refs, n_steps, pps, page):
    del pt_ref
    k_refs, v_refs = refs[:pps], refs[pps:2 * pps]
    lp_ref, g_ref, o_ref, m_sc, l_sc, acc_sc, o_sc = refs[2 * pps:]
    step = pl.program_id(1)
    q = q_ref[0] * (QK_DIM ** -0.5)

    @pl.when(step == 0)
    def _():
        s_self = jnp.sum(q * kn_ref[0], axis=1, keepdims=True)
        m_sc[...] = jnp.broadcast_to(s_self, m_sc.shape)
        l_sc[...] = jnp.ones_like(l_sc)
        acc_sc[...] = vn_ref[0]

    qb = qb_ref[0]
    slot_head = lax.broadcasted_iota(jnp.int32, (N_SLOTS, LANES), 0) // 2
    for i in range(pps):
        st = jnp.sum(k_refs[i][0] * qb, axis=1) * (QK_DIM ** -0.5)
        m_old = m_sc[...]
        m_new = jnp.maximum(m_old, jnp.max(st, axis=1, keepdims=True))
        alpha = jnp.exp(m_old - m_new)
        p = jnp.exp(st - m_new)
        l_sc[...] = alpha * l_sc[...] + jnp.sum(p, axis=1, keepdims=True)
        m_sc[...] = m_new
        p_bf = p.astype(BF16)
        pv = jnp.zeros((N_SLOTS, ATT_HEAD_DIM), F32)
        for h in range(N_ATT_HEADS):
            v_h = v_refs[i][0, pl.ds(h, page, stride=N_ATT_HEADS), :].astype(BF16)
            full = jnp.dot(p_bf, v_h, preferred_element_type=F32)
            pv = jnp.where(slot_head == h, full, pv)
        acc_sc[...] = alpha * acc_sc[...] + pv

    @pl.when(step == n_steps - 1)
    def _():
        lam = _diff_lambda(lp_ref[...])
        o_sc[...] = acc_sc[...] / l_sc[...]
        o1 = o_sc[pl.ds(0, N_ATT_HEADS, stride=2), :]
        o2 = o_sc[pl.ds(1, N_ATT_HEADS, stride=2), :]
        att = o1 - lam * o2
        ms = jnp.mean(att * att, axis=1, keepdims=True)
        o_ref[0] = att * lax.rsqrt(ms + LN_EPS) * g_ref[...] * (1.0 - LAM_INIT)


def _attn_sample(page_table, q, k_new, v_new2, cache_kt, cache_v2, lam_p, subln_g):
    nb, n_pages = page_table.shape
    pps = PAGES_PER_STEP
    page = cache_kt.shape[3]
    qb = jnp.broadcast_to(q[..., None], (*q.shape, page))
    n_steps = n_pages // pps
    kspec = lambda i: pl.BlockSpec((1, N_SLOTS, QK_DIM, page),
                                   lambda b, s, pt: (pt[b, s * pps + i], 0, 0, 0))
    vspec = lambda i: pl.BlockSpec((1, page * N_ATT_HEADS, ATT_HEAD_DIM),
                                   lambda b, s, pt: (pt[b, s * pps + i], 0, 0))
    grid_spec = pltpu.PrefetchScalarGridSpec(
        num_scalar_prefetch=1,
        grid=(nb, n_steps),
        in_specs=[
            pl.BlockSpec((1, N_SLOTS, QK_DIM), lambda b, s, pt: (b, 0, 0)),
            pl.BlockSpec((1, N_SLOTS, QK_DIM, page), lambda b, s, pt: (b, 0, 0, 0)),
            pl.BlockSpec((1, N_SLOTS, QK_DIM), lambda b, s, pt: (b, 0, 0)),
            pl.BlockSpec((1, N_SLOTS, ATT_HEAD_DIM), lambda b, s, pt: (b, 0, 0)),
            *[kspec(i) for i in range(pps)],
            *[vspec(i) for i in range(pps)],
            pl.BlockSpec((4, QK_DIM), lambda b, s, pt: (0, 0)),
            pl.BlockSpec((1, ATT_HEAD_DIM), lambda b, s, pt: (0, 0)),
        ],
        out_specs=pl.BlockSpec((1, N_ATT_HEADS, ATT_HEAD_DIM), lambda b, s, pt: (b, 0, 0)),
        scratch_shapes=[
            pltpu.VMEM((N_SLOTS, LANES), F32),
            pltpu.VMEM((N_SLOTS, LANES), F32),
            pltpu.VMEM((N_SLOTS, ATT_HEAD_DIM), F32),
            pltpu.VMEM((N_SLOTS, ATT_HEAD_DIM), F32),
        ],
    )
    return pl.pallas_call(
        functools.partial(_attn_sample_kernel, n_steps=n_steps, pps=pps, page=page),
        grid_spec=grid_spec,
        out_shape=jax.ShapeDtypeStruct((nb, N_ATT_HEADS, ATT_HEAD_DIM), F32),
        compiler_params=_cparams(("parallel", "arbitrary")),
        name="diff_attn_sample",
    )(page_table, q, qb, k_new, v_new2, *([cache_kt] * pps), *([cache_v2] * pps), lam_p, subln_g)


def _outproj_kernel(rec_ref, att_ref, wr_ref, wa_ref, x_ref, g_ref, b_ref, o_ref, ob_ref):
    mix = jnp.dot(rec_ref[...], wr_ref[...], preferred_element_type=F32)
    mix = mix + jnp.dot(att_ref[...], wa_ref[...], preferred_element_type=F32)
    y = _layer_norm(ALPHA * x_ref[...] + mix, g_ref[...], b_ref[...])
    o_ref[...] = y
    ob_ref[...] = y.astype(BF16)


def _outproj(rec_o, att_o, w_rec, w_att, x, g, b, *, tm):
    n = x.shape[0]
    return pl.pallas_call(
        _outproj_kernel,
        grid=(n // tm,),
        in_specs=[
            pl.BlockSpec((tm, D_REC), lambda i: (i, 0)),
            pl.BlockSpec((tm, D_ATT), lambda i: (i, 0)),
            pl.BlockSpec((D_REC, D_MODEL), lambda i: (0, 0)),
            pl.BlockSpec((D_ATT, D_MODEL), lambda i: (0, 0)),
            pl.BlockSpec((tm, D_MODEL), lambda i: (i, 0)),
            pl.BlockSpec((1, D_MODEL), lambda i: (0, 0)),
            pl.BlockSpec((1, D_MODEL), lambda i: (0, 0)),
        ],
        out_specs=[
            pl.BlockSpec((tm, D_MODEL), lambda i: (i, 0)),
            pl.BlockSpec((tm, D_MODEL), lambda i: (i, 0)),
        ],
        out_shape=[
            jax.ShapeDtypeStruct((n, D_MODEL), F32),
            jax.ShapeDtypeStruct((n, D_MODEL), BF16),
        ],
        compiler_params=_cparams(("parallel",)),
        name="out_proj_ln",
    )(rec_o, att_o, w_rec, w_att, x, g, b)


def _peer_scores_kernel(x_ref, wq_ref, k1_ref, k2_ref, o_ref):
    q = jnp.dot(x_ref[...], wq_ref[...], preferred_element_type=F32).astype(BF16)
    for h in range(PEER_HEADS):
        for c, k_ref in enumerate((k1_ref, k2_ref)):
            qb = q[:, (2 * h + c) * N_KEYS:(2 * h + c + 1) * N_KEYS]
            base = (c * PEER_HEADS + h) * N_KEYS
            o_ref[base:base + N_KEYS, :] = _nt_dot(k_ref[h], qb)


def _peer_scores(x1_bf, wq_bf, k1_bf, k2_bf, *, tm):
    n = x1_bf.shape[0]
    rows = 2 * PEER_HEADS * N_KEYS
    return pl.pallas_call(
        _peer_scores_kernel,
        grid=(n // tm,),
        in_specs=[
            pl.BlockSpec((tm, D_MODEL), lambda i: (i, 0)),
            pl.BlockSpec((D_MODEL, rows), lambda i: (0, 0)),
            pl.BlockSpec((PEER_HEADS, N_KEYS, N_KEYS), lambda i: (0, 0, 0)),
            pl.BlockSpec((PEER_HEADS, N_KEYS, N_KEYS), lambda i: (0, 0, 0)),
        ],
        out_specs=pl.BlockSpec((rows, tm), lambda i: (0, i)),
        out_shape=jax.ShapeDtypeStruct((rows, n), F32),
        compiler_params=_cparams(("parallel",)),
        name="peer_scores",
    )(x1_bf, wq_bf, k1_bf, k2_bf)


def _top_values(x, k):
    vals = []
    for _ in range(k):
        cur = jnp.max(x, axis=0, keepdims=True)
        vals.append(cur)
        x = jnp.where(x == cur, -jnp.inf, x)
    return jnp.concatenate(vals, axis=0)


def _peer_stats_kernel(s_ref, e1_ref, e2_ref, cnt_ref, rank_ref):
    half = PEER_HEADS * N_KEYS
    k = PEER_TOPK
    for h in range(PEER_HEADS):
        r1 = slice(h * N_KEYS, (h + 1) * N_KEYS)
        s1 = s_ref[r1, :]
        s2 = s_ref[half + h * N_KEYS:half + (h + 1) * N_KEYS, :]
        v1 = _top_values(s1, k)
        v2 = _top_values(s2, k)
        sub = lax.broadcasted_iota(jnp.int32, (8, 1), 0)
        cands = [v1[0:8] + v2[0:1], v1[8:16] + v2[0:1], v2[8:16] + v1[0:1]]
        for b in range(1, 8):
            n_a = k // (b + 1)
            cands.append(jnp.where(sub < n_a, v1[0:8] + v2[b:b + 1], -jnp.inf))
        cand = jnp.concatenate(cands, axis=0)
        work = cand
        theta = None
        for _ in range(k):
            theta = jnp.max(work, axis=0, keepdims=True)
            work = jnp.where(work == theta, -jnp.inf, work)
        m = v1[0:1] + v2[0:1]
        z = jnp.sum(jnp.where(cand >= theta, jnp.exp(cand - m), 0.0), axis=0, keepdims=True)
        e1_ref[r1, :] = jnp.exp(s1 - (v1[0:1] + jnp.log(z)))
        e2_ref[r1, :] = jnp.exp(s2 - v2[0:1]).astype(BF16)
        cnt = jnp.zeros(s1.shape, F32)
        rank = jnp.zeros(s2.shape, F32)
        for b in range(k):
            cnt = cnt + jnp.where(s1 + v2[b:b + 1] >= theta, 1.0, 0.0)
            rank = rank + jnp.where(v2[b:b + 1] > s2, 1.0, 0.0)
        cnt_ref[r1, :] = cnt
        rank_ref[r1, :] = rank.astype(BF16)


def _peer_stats(st, *, tl):
    rows, n = st.shape
    half = rows // 2
    spec = pl.BlockSpec((half, tl), lambda i: (0, i))
    f32, bf16 = jax.ShapeDtypeStruct((half, n), F32), jax.ShapeDtypeStruct((half, n), BF16)
    return pl.pallas_call(
        _peer_stats_kernel,
        grid=(n // tl,),
        in_specs=[pl.BlockSpec((rows, tl), lambda i: (0, i))],
        out_specs=[spec] * 4,
        out_shape=[f32, bf16, f32, bf16],
        compiler_params=_cparams(("parallel",)),
        name="peer_stats",
    )(st)


PEER_CHUNK = 8 * N_KEYS
PEER_SUB = 2 * N_KEYS
GATE_ROWS = 64


def _peer_dense_kernel(xt_ref, u_ref, vt_ref, rank_ref, e2_ref, e1_ref, cnt_ref, x_ref, g_ref,
                       b_ref, o_ref, acc_ref, sc_ref, p_ref, *, n_chunks):
    j = pl.program_id(1)
    tb = p_ref.shape[1]

    @pl.when(j == 0)
    def _():
        acc_ref[...] = jnp.zeros_like(acc_ref)

    xt = xt_ref[...]
    n_sub = PEER_CHUNK // PEER_SUB
    subs = [slice(c * PEER_SUB, (c + 1) * PEER_SUB) for c in range(n_sub)]
    sc_ref[subs[0], :] = jnp.dot(u_ref[subs[0], :], xt, preferred_element_type=F32)
    for c in range(n_sub):
        if c + 1 < n_sub:
            sc_ref[subs[c + 1], :] = jnp.dot(u_ref[subs[c + 1], :], xt, preferred_element_type=F32)
        for r in range(c * (PEER_SUB // N_KEYS), (c + 1) * (PEER_SUB // N_KEYS)):
            for i2h, lg in itertools.product(range(N_KEYS // GATE_ROWS), range(tb // LANES)):
                cols = slice(lg * LANES, (lg + 1) * LANES)
                rows = slice(r * N_KEYS + i2h * GATE_ROWS, r * N_KEYS + (i2h + 1) * GATE_ROWS)
                w = None
                for h in range(PEER_HEADS):
                    hr = slice(h * N_KEYS + i2h * GATE_ROWS, h * N_KEYS + (i2h + 1) * GATE_ROWS)
                    cnt = cnt_ref[h, r:r + 1, cols].astype(BF16)
                    e1 = e1_ref[h, r:r + 1, cols].astype(BF16)
                    gate = jnp.where(rank_ref[hr, cols] < cnt, e2_ref[hr, cols] * e1,
                                     jnp.zeros((), BF16))
                    w = gate if w is None else w + gate
                p_ref[rows, cols] = w * _gelu(sc_ref[rows, cols]).astype(BF16)
        acc_ref[...] += jnp.dot(vt_ref[:, subs[c]], p_ref[subs[c], :], preferred_element_type=F32)

    @pl.when(j == n_chunks - 1)
    def _():
        o_ref[...] = _layer_norm(ALPHA * x_ref[...] + acc_ref[...].T, g_ref[...], b_ref[...])


def _peer_dense(x1t_bf, u_bf, vt_bf, e1t, e2t, cntt, rankt, x1, g, b, *, tb):
    n = x1.shape[0]
    e = PEER_CHUNK
    n_chunks = N_EXPERTS // e
    rpc = e // N_KEYS
    half = PEER_HEADS * N_KEYS
    tok = pl.BlockSpec((half, tb), lambda i, j: (0, i))
    by_head = lambda a: a.reshape(PEER_HEADS, N_KEYS, n)
    rowspec = pl.BlockSpec((PEER_HEADS, rpc, tb), lambda i, j: (0, j, i))
    return pl.pallas_call(
        functools.partial(_peer_dense_kernel, n_chunks=n_chunks),
        grid=(n // tb, n_chunks),
        in_specs=[
            pl.BlockSpec((D_MODEL, tb), lambda i, j: (0, i)),
            pl.BlockSpec((e, D_MODEL), lambda i, j: (j, 0)),
            pl.BlockSpec((D_MODEL, e), lambda i, j: (0, j)),
            tok, tok,
            rowspec, rowspec,
            pl.BlockSpec((tb, D_MODEL), lambda i, j: (i, 0), pipeline_mode=pl.Buffered(1)),
            pl.BlockSpec((1, D_MODEL), lambda i, j: (0, 0)),
            pl.BlockSpec((1, D_MODEL), lambda i, j: (0, 0)),
        ],
        out_specs=pl.BlockSpec((tb, D_MODEL), lambda i, j: (i, 0)),
        out_shape=jax.ShapeDtypeStruct((n, D_MODEL), F32),
        scratch_shapes=[
            pltpu.VMEM((D_MODEL, tb), F32),
            pltpu.VMEM((e, tb), F32),
            pltpu.VMEM((e, tb), BF16),
        ],
        compiler_params=_cparams(("parallel", "arbitrary")),
        name="peer_dense",
    )(x1t_bf, u_bf, vt_bf, rankt, e2t, by_head(e1t), by_head(cntt), x1, g, b)


def _rope_tables(pos):
    half = QK_DIM // 2
    inv = ROPE_THETA ** (-jnp.arange(half, dtype=F32) * 2.0 / QK_DIM)
    ang = pos.astype(F32)[:, None] * inv[None, :]
    cos, sin = jnp.cos(ang), jnp.sin(ang)
    reps = LANES // QK_DIM
    return (jnp.tile(jnp.concatenate([cos, cos], axis=1), (1, reps)),
            jnp.tile(jnp.concatenate([-sin, sin], axis=1), (1, reps)))


def _peer_mixer(x1, x1_bf, w, *, tm, tl, tb):
    st = _peer_scores(x1_bf, w["wq"], w["k1"], w["k2"], tm=tm)
    e1t, e2t, cntt, rankt = _peer_stats(st, tl=tl)
    return _peer_dense(x1_bf.T, w["u"], w["vt"], e1t, e2t, cntt, rankt, x1, w["ln2_g"], w["ln2_b"],
                       tb=tb)


def kernel(x_prompt, x_sample, cache_k, cache_v, state_conv, state_h, page_table, w_in, conv_w, conv_b, lru_wa, lru_ba, lru_wx, lru_bx, lru_lambda, lambda_q1, lambda_k1, lambda_q2, lambda_k2, subln_g, w_out, ln1_g, ln1_b, peer_wq, peer_k1, peer_k2, peer_u, peer_v, ln2_g, ln2_b):
    bsz, seq, _ = x_prompt.shape
    dbsz, dseq, _ = x_sample.shape
    n_pages = page_table.shape[1]
    page = cache_k.shape[2]
    past = n_pages * page
    n = bsz * seq
    l = 0

    w_in_bf = w_in[l].astype(BF16)
    w_rec = w_in_bf[:, :2 * D_REC]
    w_q = w_in_bf[:, 2 * D_REC:2 * D_REC + D_ATT]
    w_k = w_in_bf[:, 2 * D_REC + D_ATT:2 * D_REC + 2 * D_ATT]
    w_v = w_in_bf[:, 2 * D_REC + 2 * D_ATT:]
    w_out_bf = w_out[l].astype(BF16)
    wa_bf, wx_bf = lru_wa[l].astype(BF16), lru_wx[l].astype(BF16)
    row = lambda a: a[l].reshape(1, -1)
    lam_p = jnp.stack([lambda_q1[l], lambda_k1[l], lambda_q2[l], lambda_k2[l]])
    peer_w = dict(wq=peer_wq[l].astype(BF16), k1=peer_k1[l].astype(BF16), k2=peer_k2[l].astype(BF16),
                  u=peer_u[l].astype(BF16), vt=peer_v[l].T.astype(BF16),
                  ln2_g=row(ln2_g), ln2_b=row(ln2_b))

    xp = x_prompt.reshape(n, D_MODEL)
    xp_bf = xp.astype(BF16)
    cos_p, sin_p = _rope_tables(jnp.arange(seq, dtype=jnp.int32))
    proj = functools.partial(_proj, xp_bf, cos_t=cos_p, sin_t=sin_p, tm=1024, tn=512)
    rec_p = proj(w_rec, rope=False, out_dtype=F32)
    q_p = proj(w_q, rope=True, out_dtype=BF16)
    k_p = proj(w_k, rope=True, out_dtype=F32)
    v_p = proj(w_v, rope=False, out_dtype=F32)
    rec_out_p, h_p = _rec_prompt(rec_p, conv_w[l], row(conv_b), wa_bf, wx_bf, row(lru_ba),
                                 row(lru_bx), row(lru_lambda), bsz=bsz, t=seq, cb=256)
    att_p = _attn_prompt(q_p, k_p, v_p, lam_p, row(subln_g), bsz=bsz, t=seq, tq=512)
    x1_p, x1_p_bf = _outproj(rec_out_p, att_p, w_out_bf[:D_REC], w_out_bf[D_REC:], xp,
                             row(ln1_g), row(ln1_b), tm=256)
    y_p = _peer_mixer(x1_p, x1_p_bf, peer_w, tm=256, tl=256, tb=512)

    xs = x_sample.reshape(dbsz * dseq, D_MODEL)
    ns = xs.shape[0]
    xs_bf = xs.astype(BF16)
    cos_s, sin_s = _rope_tables(jnp.full((ns,), past, dtype=jnp.int32))
    proj_s = functools.partial(_proj, xs_bf, cos_t=cos_s, sin_t=sin_s, tm=ns, tn=512)
    rec_s = proj_s(w_rec, rope=False, out_dtype=F32)
    q_s = proj_s(w_q, rope=True, out_dtype=F32)
    k_s = proj_s(w_k, rope=True, out_dtype=F32)
    v_s = proj_s(w_v, rope=False, out_dtype=F32)
    rec_out_s, conv_s_t, h_s = _rec_sample(
        rec_s, state_conv[l].transpose(1, 0, 2), state_h[l], conv_w[l], row(conv_b), wa_bf, wx_bf,
        row(lru_ba), row(lru_bx), row(lru_lambda))
    n_pool = cache_k.shape[1]
    att_s = _attn_sample(
        page_table, q_s.reshape(ns, N_SLOTS, QK_DIM), k_s.reshape(ns, N_SLOTS, QK_DIM),
        jnp.repeat(v_s.reshape(ns, N_ATT_HEADS, ATT_HEAD_DIM), 2, axis=1),
        cache_k[l].transpose(0, 2, 3, 1),
        cache_v[l].reshape(n_pool, page * N_ATT_HEADS, ATT_HEAD_DIM), lam_p, row(subln_g))
    pad = LANES - ns
    padr = lambda a: jnp.pad(a, ((0, pad), (0, 0)))
    x1_s, x1_s_bf = _outproj(padr(rec_out_s), padr(att_s.reshape(ns, D_ATT).astype(BF16)), w_out_bf[:D_REC],
                             w_out_bf[D_REC:], padr(xs), row(ln1_g), row(ln1_b), tm=LANES)
    y_s = _peer_mixer(x1_s, x1_s_bf, peer_w, tm=LANES, tl=LANES, tb=LANES)[:ns]

    k4 = (2 * N_ATT_HEADS, QK_DIM)
    v4 = (N_ATT_HEADS, ATT_HEAD_DIM)
    return (
        y_p.reshape(bsz, seq, D_MODEL),
        y_s.reshape(dbsz, dseq, D_MODEL),
        k_p.reshape(1, bsz, seq, *k4),
        v_p.reshape(1, bsz, seq, *v4),
        rec_p.reshape(bsz, seq, 2 * D_REC)[:, seq - (CONV_W - 1):, :D_REC][None],
        h_p.reshape(1, bsz, D_REC),
        k_s.reshape(1, dbsz, dseq, *k4),
        v_s.reshape(1, dbsz, dseq, *v4),
        conv_s_t.transpose(1, 0, 2)[None],
        h_s.reshape(1, dbsz, D_REC),
    )
```

```python
import functools
import itertools
import math

import jax
import jax.numpy as jnp
from jax import lax
from jax.experimental import pallas as pl
from jax.experimental.pallas import tpu as pltpu

F32 = jnp.float32
BF16 = jnp.bfloat16

D_MODEL = 2048
D_REC = 1024
D_ATT = 1024
N_REC_BLOCKS = 8
REC_BLOCK = 128
CONV_W = 4
LRU_C = 8.0
N_ATT_HEADS = 8
ATT_HEAD_DIM = 128
QK_DIM = 64
ROPE_THETA = 10000.0
N_KEYS = 128
N_EXPERTS = N_KEYS * N_KEYS
PEER_HEADS = 8
PEER_TOPK = 16
LN_EPS = 1e-5
NEG_INF = -1e30
DEPTH = 1
ALPHA = (2.0 * DEPTH) ** 0.25
LAM_INIT = 0.8 - 0.6 * math.exp(-0.3 * 0)
LANES = 128

VMEM_LIMIT = 56 * 1024 * 1024


def _cparams(sem, flags=None):
    return pltpu.CompilerParams(dimension_semantics=sem, vmem_limit_bytes=VMEM_LIMIT, flags=flags)


def _gelu(x):
    c = math.sqrt(2.0 / math.pi)
    return x * (0.5 * (1.0 + jnp.tanh(c * (x + 0.044715 * (x * x * x)))))


def _layer_norm(y, g, b):
    mu = jnp.mean(y, axis=-1, keepdims=True)
    yc = y - mu
    var = jnp.mean(yc * yc, axis=-1, keepdims=True)
    return yc * lax.rsqrt(var + LN_EPS) * g + b


def _proj_kernel(x_ref, w_ref, cos_ref, sin_ref, o_ref, *, rope):
    acc = jnp.dot(x_ref[...], w_ref[...], preferred_element_type=F32)
    if rope:
        c = cos_ref[...]
        s = sin_ref[...]
        lane = lax.broadcasted_iota(jnp.int32, (1, LANES), 1)
        first_half = (lane % QK_DIM) < (QK_DIM // 2)
        outs = []
        for j in range(acc.shape[1] // LANES):
            xc = acc[:, j * LANES:(j + 1) * LANES]
            fwd = pltpu.roll(xc, LANES - QK_DIM // 2, axis=1)
            bwd = pltpu.roll(xc, QK_DIM // 2, axis=1)
            partner = jnp.where(first_half, fwd, bwd)
            outs.append(xc * c + partner * s)
        acc = jnp.concatenate(outs, axis=1)
    o_ref[...] = acc.astype(o_ref.dtype)


def _proj(x_bf, w_bf, cos_t, sin_t, *, rope, out_dtype, tm, tn):
    n, k = x_bf.shape
    m = w_bf.shape[1]
    n_pos_blocks = cos_t.shape[0] // tm
    return pl.pallas_call(
        functools.partial(_proj_kernel, rope=rope),
        grid=(n // tm, m // tn),
        in_specs=[
            pl.BlockSpec((tm, k), lambda i, j: (i, 0)),
            pl.BlockSpec((k, tn), lambda i, j: (0, j)),
            pl.BlockSpec((tm, LANES), lambda i, j: (i % n_pos_blocks, 0)),
            pl.BlockSpec((tm, LANES), lambda i, j: (i % n_pos_blocks, 0)),
        ],
        out_specs=pl.BlockSpec((tm, tn), lambda i, j: (i, j)),
        out_shape=jax.ShapeDtypeStruct((n, m), out_dtype),
        compiler_params=_cparams(("parallel", "parallel")),
        name="in_proj_rope" if rope else "in_proj",
    )(x_bf, w_bf, cos_t, sin_t)


def _lru_gates(conv, wa_ref, wx_ref, ba, bx, lam):
    rs, is_ = [], []
    for blk in range(conv.shape[1] // REC_BLOCK):
        cb = conv[:, blk * REC_BLOCK:(blk + 1) * REC_BLOCK].astype(BF16)
        rs.append(jnp.dot(cb, wa_ref[blk], preferred_element_type=F32))
        is_.append(jnp.dot(cb, wx_ref[blk], preferred_element_type=F32))
    r = jax.nn.sigmoid(jnp.concatenate(rs, axis=1) + ba)
    i = jax.nn.sigmoid(jnp.concatenate(is_, axis=1) + bx)
    softplus_neg_lam = jnp.maximum(-lam, 0.0) + jnp.log1p(jnp.exp(-jnp.abs(lam)))
    log_a = -LRU_C * r * softplus_neg_lam
    a = jnp.exp(log_a)
    b = jnp.sqrt(1.0 - jnp.exp(2.0 * log_a)) * (i * conv)
    return a, b


def _shift_rows(x, s, fill):
    row = lax.broadcasted_iota(jnp.int32, x.shape, 0)
    return jnp.where(row >= s, pltpu.roll(x, s, axis=0), fill)


def _rec_prompt_kernel(x_ref, g_ref, cw_ref, cb_ref, wa_ref, wx_ref, ba_ref, bx_ref, lam_ref,
                       o_ref, h_ref):
    x = x_ref[...]
    t = x.shape[0]
    cw = cw_ref[...]
    conv = (cb_ref[...] + cw[3:4] * x + cw[2:3] * _shift_rows(x, 1, 0.0)
            + cw[1:2] * _shift_rows(x, 2, 0.0) + cw[0:1] * _shift_rows(x, 3, 0.0))
    a, b = _lru_gates(conv, wa_ref, wx_ref, ba_ref[...], bx_ref[...], lam_ref[...])
    s = 1
    while s < t:
        b = a * _shift_rows(b, s, 0.0) + b
        a = a * _shift_rows(a, s, 1.0)
        s *= 2
    o_ref[...] = (b * _gelu(g_ref[...])).astype(o_ref.dtype)
    h_ref[0] = b[t - 1:t, :]


def _rec_prompt(rec, conv_w, conv_b, wa_bf, wx_bf, ba, bx, lam, *, bsz, t, cb):
    ncb = D_REC // cb
    return pl.pallas_call(
        _rec_prompt_kernel,
        grid=(bsz, ncb),
        in_specs=[
            pl.BlockSpec((t, cb), lambda b, c: (b, c)),
            pl.BlockSpec((t, cb), lambda b, c: (b, ncb + c)),
            pl.BlockSpec((CONV_W, cb), lambda b, c: (0, c)),
            pl.BlockSpec((1, cb), lambda b, c: (0, c)),
            pl.BlockSpec((cb // REC_BLOCK, REC_BLOCK, REC_BLOCK), lambda b, c: (c, 0, 0)),
            pl.BlockSpec((cb // REC_BLOCK, REC_BLOCK, REC_BLOCK), lambda b, c: (c, 0, 0)),
            pl.BlockSpec((1, cb), lambda b, c: (0, c)),
            pl.BlockSpec((1, cb), lambda b, c: (0, c)),
            pl.BlockSpec((1, cb), lambda b, c: (0, c)),
        ],
        out_specs=[
            pl.BlockSpec((t, cb), lambda b, c: (b, c)),
            pl.BlockSpec((1, 1, cb), lambda b, c: (b, 0, c)),
        ],
        out_shape=[
            jax.ShapeDtypeStruct((bsz * t, D_REC), BF16),
            jax.ShapeDtypeStruct((bsz, 1, D_REC), F32),
        ],
        compiler_params=_cparams(("parallel", "parallel")),
        name="rglru_prompt",
    )(rec, rec, conv_w, conv_b, wa_bf, wx_bf, ba, bx, lam)


def _rec_sample_kernel(rec_ref, sc_ref, h0_ref, cw_ref, cb_ref, wa_ref, wx_ref, ba_ref, bx_ref,
                       lam_ref, o_ref, nc_ref, h_ref):
    x = rec_ref[:, :D_REC]
    g = rec_ref[:, D_REC:]
    cw = cw_ref[...]
    conv = (cb_ref[...] + cw[0:1] * sc_ref[0] + cw[1:2] * sc_ref[1] + cw[2:3] * sc_ref[2]
            + cw[3:4] * x)
    a, b = _lru_gates(conv, wa_ref, wx_ref, ba_ref[...], bx_ref[...], lam_ref[...])
    h = a * h0_ref[...] + b
    o_ref[...] = (h * _gelu(g)).astype(o_ref.dtype)
    nc_ref[0] = sc_ref[1]
    nc_ref[1] = sc_ref[2]
    nc_ref[2] = x
    h_ref[...] = h


def _rec_sample(rec, sc_t, h0, conv_w, conv_b, wa_bf, wx_bf, ba, bx, lam):
    n = rec.shape[0]
    return pl.pallas_call(
        _rec_sample_kernel,
        out_shape=[
            jax.ShapeDtypeStruct((n, D_REC), BF16),
            jax.ShapeDtypeStruct((CONV_W - 1, n, D_REC), F32),
            jax.ShapeDtypeStruct((n, D_REC), F32),
        ],
        compiler_params=pltpu.CompilerParams(vmem_limit_bytes=VMEM_LIMIT),
        name="rglru_sample",
    )(rec, sc_t, h0, conv_w, conv_b, wa_bf, wx_bf, ba, bx, lam)


def _diff_lambda(lp):
    d1 = jnp.sum(lp[0:1] * lp[1:2], axis=1, keepdims=True)
    d2 = jnp.sum(lp[2:3] * lp[3:4], axis=1, keepdims=True)
    return jnp.exp(d1) - jnp.exp(d2) + LAM_INIT


def _nt_dot(a, b):
    return lax.dot_general(a, b, (((1,), (1,)), ((), ())), preferred_element_type=F32)


def _attn_prompt_kernel(q_ref, k_ref, v_ref, lp_ref, g_ref, o_ref, m_sc, l_sc, acc_sc, *, tq):
    qi = pl.program_id(2)
    lane = lax.broadcasted_iota(jnp.int32, (1, LANES), 1)
    qs = q_ref[...] * (QK_DIM ** -0.5)
    zero = jnp.zeros_like(qs)
    qmaps = (jnp.where(lane < QK_DIM, qs, zero), jnp.where(lane >= QK_DIM, qs, zero))
    m_sc[...] = jnp.full(m_sc.shape, NEG_INF, F32)
    l_sc[...] = jnp.zeros_like(l_sc)
    acc_sc[...] = jnp.zeros_like(acc_sc)
    reps = tq // LANES

    def block(kb, masked):
        start = pl.multiple_of(kb * tq, tq)
        k = k_ref[pl.ds(start, tq), :].astype(BF16)
        v = v_ref[pl.ds(start, tq), :].astype(BF16)
        for c in range(2):
            s = _nt_dot(qmaps[c], k)
            if masked:
                row = lax.broadcasted_iota(jnp.int32, s.shape, 0)
                col = lax.broadcasted_iota(jnp.int32, s.shape, 1)
                s = jnp.where(col <= row, s, NEG_INF)
            m_old = m_sc[c]
            m_new = jnp.maximum(m_old, jnp.max(s, axis=1, keepdims=True))
            alpha = jnp.exp(m_old - m_new)
            p = jnp.exp(s - jnp.tile(m_new, (1, reps)))
            l_sc[c] = alpha * l_sc[c] + jnp.sum(p, axis=1, keepdims=True)
            acc_sc[c] = alpha * acc_sc[c] + jnp.dot(p.astype(BF16), v, preferred_element_type=F32)
            m_sc[c] = m_new

    @pl.loop(0, qi)
    def _(kb):
        block(kb, False)

    block(qi, True)
    lam = _diff_lambda(lp_ref[...])
    att = acc_sc[0] / l_sc[0] - lam * (acc_sc[1] / l_sc[1])
    ms = jnp.mean(att * att, axis=1, keepdims=True)
    att = att * lax.rsqrt(ms + LN_EPS) * g_ref[...] * (1.0 - LAM_INIT)
    o_ref[...] = att.astype(o_ref.dtype)


def _attn_prompt(q_bf, k, v, lam_p, subln_g, *, bsz, t, tq):
    nq = t // tq
    return pl.pallas_call(
        functools.partial(_attn_prompt_kernel, tq=tq),
        grid=(bsz, N_ATT_HEADS, nq),
        in_specs=[
            pl.BlockSpec((tq, LANES), lambda b, h, i: (b * nq + i, h)),
            pl.BlockSpec((t, LANES), lambda b, h, i: (b, h)),
            pl.BlockSpec((t, LANES), lambda b, h, i: (b, h)),
            pl.BlockSpec((4, QK_DIM), lambda b, h, i: (0, 0)),
            pl.BlockSpec((1, ATT_HEAD_DIM), lambda b, h, i: (0, 0)),
        ],
        out_specs=pl.BlockSpec((tq, LANES), lambda b, h, i: (b * nq + i, h)),
        out_shape=jax.ShapeDtypeStruct((bsz * t, D_ATT), BF16),
        scratch_shapes=[
            pltpu.VMEM((2, tq, LANES), F32),
            pltpu.VMEM((2, tq, LANES), F32),
            pltpu.VMEM((2, tq, ATT_HEAD_DIM), F32),
        ],
        compiler_params=_cparams(("parallel", "parallel", "arbitrary")),
        name="diff_attn_prompt",
    )(q_bf, k, v, lam_p, subln_g)


N_SLOTS = 2 * N_ATT_HEADS
PAGES_PER_STEP = 8


def _attn_sample_kernel(pt_ref, q_ref, qb_ref, kn_ref, vn_ref, *refs, n_steps, pps, page):
    del pt_ref
    k_refs, v_refs = refs[:pps], refs[pps:2 * pps]
    lp_ref, g_ref, o_ref, m_sc, l_sc, acc_sc, o_sc = refs[2 * pps:]
    step = pl.program_id(1)
    q = q_ref[0] * (QK_DIM ** -0.5)

    @pl.when(step == 0)
    def _():
        s_self = jnp.sum(q * kn_ref[0], axis=1, keepdims=True)
        m_sc[...] = jnp.broadcast_to(s_self, m_sc.shape)
        l_sc[...] = jnp.ones_like(l_sc)
        acc_sc[...] = vn_ref[0]

    qb = qb_ref[0]
    slot_head = lax.broadcasted_iota(jnp.int32, (N_SLOTS, LANES), 0) // 2
    for i in range(pps):
        st = jnp.sum(k_refs[i][0] * qb, axis=1) * (QK_DIM ** -0.5)
        m_old = m_sc[...]
        m_new = jnp.maximum(m_old, jnp.max(st, axis=1, keepdims=True))
        alpha = jnp.exp(m_old - m_new)
        p = jnp.exp(st - m_new)
        l_sc[...] = alpha * l_sc[...] + jnp.sum(p, axis=1, keepdims=True)
        m_sc[...] = m_new
        p_bf = p.astype(BF16)
        pv = jnp.zeros((N_SLOTS, ATT_HEAD_DIM), F32)
        for h in range(N_ATT_HEADS):
            v_h = v_refs[i][0, pl.ds(h, page, stride=N_ATT_HEADS), :].astype(BF16)
            full = jnp.dot(p_bf, v_h, preferred_element_type=F32)
            pv = jnp.where(slot_head == h, full, pv)
        acc_sc[...] = alpha * acc_sc[...] + pv

    @pl.when(step == n_steps - 1)
    def _():
        lam = _diff_lambda(lp_ref[...])
        o_sc[...] = acc_sc[...] / l_sc[...]
        o1 = o_sc[pl.ds(0, N_ATT_HEADS, stride=2), :]
        o2 = o_sc[pl.ds(1, N_ATT_HEADS, stride=2), :]
        att = o1 - lam * o2
        ms = jnp.mean(att * att, axis=1, keepdims=True)
        o_ref[0] = att * lax.rsqrt(ms + LN_EPS) * g_ref[...] * (1.0 - LAM_INIT)


def _attn_sample(page_table, q, k_new, v_new2, cache_kt, cache_v2, lam_p, subln_g):
    nb, n_pages = page_table.shape
    pps = PAGES_PER_STEP
    page = cache_kt.shape[3]
    qb = jnp.broadcast_to(q[..., None], (*q.shape, page))
    n_steps = n_pages // pps
    kspec = lambda i: pl.BlockSpec((1, N_SLOTS, QK_DIM, page),
                                   lambda b, s, pt: (pt[b, s * pps + i], 0, 0, 0))
    vspec = lambda i: pl.BlockSpec((1, page * N_ATT_HEADS, ATT_HEAD_DIM),
                                   lambda b, s, pt: (pt[b, s * pps + i], 0, 0))
    grid_spec = pltpu.PrefetchScalarGridSpec(
        num_scalar_prefetch=1,
        grid=(nb, n_steps),
        in_specs=[
            pl.BlockSpec((1, N_SLOTS, QK_DIM), lambda b, s, pt: (b, 0, 0)),
            pl.BlockSpec((1, N_SLOTS, QK_DIM, page), lambda b, s, pt: (b, 0, 0, 0)),
            pl.BlockSpec((1, N_SLOTS, QK_DIM), lambda b, s, pt: (b, 0, 0)),
            pl.BlockSpec((1, N_SLOTS, ATT_HEAD_DIM), lambda b, s, pt: (b, 0, 0)),
            *[kspec(i) for i in range(pps)],
            *[vspec(i) for i in range(pps)],
            pl.BlockSpec((4, QK_DIM), lambda b, s, pt: (0, 0)),
            pl.BlockSpec((1, ATT_HEAD_DIM), lambda b, s, pt: (0, 0)),
        ],
        out_specs=pl.BlockSpec((1, N_ATT_HEADS, ATT_HEAD_DIM), lambda b, s, pt: (b, 0, 0)),
        scratch_shapes=[
            pltpu.VMEM((N_SLOTS, LANES), F32),
            pltpu.VMEM((N_SLOTS, LANES), F32),
            pltpu.VMEM((N_SLOTS, ATT_HEAD_DIM), F32),
            pltpu.VMEM((N_SLOTS, ATT_HEAD_DIM), F32),
        ],
    )
    return pl.pallas_call(
        functools.partial(_attn_sample_kernel, n_steps=n_steps, pps=pps, page=page),
        grid_spec=grid_spec,
        out_shape=jax.ShapeDtypeStruct((nb, N_ATT_HEADS, ATT_HEAD_DIM), F32),
        compiler_params=_cparams(("parallel", "arbitrary")),
        name="diff_attn_sample",
    )(page_table, q, qb, k_new, v_new2, *([cache_kt] * pps), *([cache_v2] * pps), lam_p, subln_g)


def _outproj_kernel(rec_ref, att_ref, wr_ref, wa_ref, x_ref, g_ref, b_ref, o_ref, ob_ref):
    mix = jnp.dot(rec_ref[...], wr_ref[...], preferred_element_type=F32)
    mix = mix + jnp.dot(att_ref[...], wa_ref[...], preferred_element_type=F32)
    y = _layer_norm(ALPHA * x_ref[...] + mix, g_ref[...], b_ref[...])
    o_ref[...] = y
    ob_ref[...] = y.astype(BF16)


def _outproj(rec_o, att_o, w_rec, w_att, x, g, b, *, tm):
    n = x.shape[0]
    return pl.pallas_call(
        _outproj_kernel,
        grid=(n // tm,),
        in_specs=[
            pl.BlockSpec((tm, D_REC), lambda i: (i, 0)),
            pl.BlockSpec((tm, D_ATT), lambda i: (i, 0)),
            pl.BlockSpec((D_REC, D_MODEL), lambda i: (0, 0)),
            pl.BlockSpec((D_ATT, D_MODEL), lambda i: (0, 0)),
            pl.BlockSpec((tm, D_MODEL), lambda i: (i, 0)),
            pl.BlockSpec((1, D_MODEL), lambda i: (0, 0)),
            pl.BlockSpec((1, D_MODEL), lambda i: (0, 0)),
        ],
        out_specs=[
            pl.BlockSpec((tm, D_MODEL), lambda i: (i, 0)),
            pl.BlockSpec((tm, D_MODEL), lambda i: (i, 0)),
        ],
        out_shape=[
            jax.ShapeDtypeStruct((n, D_MODEL), F32),
            jax.ShapeDtypeStruct((n, D_MODEL), BF16),
        ],
        compiler_params=_cparams(("parallel",)),
        name="out_proj_ln",
    )(rec_o, att_o, w_rec, w_att, x, g, b)


def _peer_scores_kernel(x_ref, wq_ref, k1_ref, k2_ref, o_ref):
    q = jnp.dot(x_ref[...], wq_ref[...], preferred_element_type=F32).astype(BF16)
    for h in range(PEER_HEADS):
        for c, k_ref in enumerate((k1_ref, k2_ref)):
            qb = q[:, (2 * h + c) * N_KEYS:(2 * h + c + 1) * N_KEYS]
            base = (c * PEER_HEADS + h) * N_KEYS
            o_ref[base:base + N_KEYS, :] = _nt_dot(k_ref[h], qb)


def _peer_scores(x1_bf, wq_bf, k1_bf, k2_bf, *, tm):
    n = x1_bf.shape[0]
    rows = 2 * PEER_HEADS * N_KEYS
    return pl.pallas_call(
        _peer_scores_kernel,
        grid=(n // tm,),
        in_specs=[
            pl.BlockSpec((tm, D_MODEL), lambda i: (i, 0)),
            pl.BlockSpec((D_MODEL, rows), lambda i: (0, 0)),
            pl.BlockSpec((PEER_HEADS, N_KEYS, N_KEYS), lambda i: (0, 0, 0)),
            pl.BlockSpec((PEER_HEADS, N_KEYS, N_KEYS), lambda i: (0, 0, 0)),
        ],
        out_specs=pl.BlockSpec((rows, tm), lambda i: (0, i)),
        out_shape=jax.ShapeDtypeStruct((rows, n), F32),
        compiler_params=_cparams(("parallel",)),
        name="peer_scores",
    )(x1_bf, wq_bf, k1_bf, k2_bf)


def _top_values(x, k):
    vals = []
    for _ in range(k):
        cur = jnp.max(x, axis=0, keepdims=True)
        vals.append(cur)
        x = jnp.where(x == cur, -jnp.inf, x)
    return jnp.concatenate(vals, axis=0)


def _peer_stats_kernel(s_ref, e1_ref, e2_ref, cnt_ref, rank_ref):
    half = PEER_HEADS * N_KEYS
    k = PEER_TOPK
    for h in range(PEER_HEADS):
        r1 = slice(h * N_KEYS, (h + 1) * N_KEYS)
        s1 = s_ref[r1, :]
        s2 = s_ref[half + h * N_KEYS:half + (h + 1) * N_KEYS, :]
        v1 = _top_values(s1, k)
        v2 = _top_values(s2, k)
        sub = lax.broadcasted_iota(jnp.int32, (8, 1), 0)
        cands = [v1[0:8] + v2[0:1], v1[8:16] + v2[0:1], v2[8:16] + v1[0:1]]
        for b in range(1, 8):
            n_a = k // (b + 1)
            cands.append(jnp.where(sub < n_a, v1[0:8] + v2[b:b + 1], -jnp.inf))
        cand = jnp.concatenate(cands, axis=0)
        work = cand
        theta = None
        for _ in range(k):
            theta = jnp.max(work, axis=0, keepdims=True)
            work = jnp.where(work == theta, -jnp.inf, work)
        m = v1[0:1] + v2[0:1]
        z = jnp.sum(jnp.where(cand >= theta, jnp.exp(cand - m), 0.0), axis=0, keepdims=True)
        e1_ref[r1, :] = jnp.exp(s1 - (v1[0:1] + jnp.log(z)))
        e2_ref[r1, :] = jnp.exp(s2 - v2[0:1]).astype(BF16)
        cnt = jnp.zeros(s1.shape, F32)
        rank = jnp.zeros(s2.shape, F32)
        for b in range(k):
            cnt = cnt + jnp.where(s1 + v2[b:b + 1] >= theta, 1.0, 0.0)
            rank = rank + jnp.where(v2[b:b + 1] > s2, 1.0, 0.0)
        cnt_ref[r1, :] = cnt
        rank_ref[r1, :] = rank.astype(BF16)


def _peer_stats(st, *, tl):
    rows, n = st.shape
    half = rows // 2
    spec = pl.BlockSpec((half, tl), lambda i: (0, i))
    f32, bf16 = jax.ShapeDtypeStruct((half, n), F32), jax.ShapeDtypeStruct((half, n), BF16)
    return pl.pallas_call(
        _peer_stats_kernel,
        grid=(n // tl,),
        in_specs=[pl.BlockSpec((rows, tl), lambda i: (0, i))],
        out_specs=[spec] * 4,
        out_shape=[f32, bf16, f32, bf16],
        compiler_params=_cparams(("parallel",)),
        name="peer_stats",
    )(st)


PEER_CHUNK = 8 * N_KEYS
PEER_SUB = 2 * N_KEYS
GATE_ROWS = 64


def _peer_dense_kernel(xt_ref, u_ref, vt_ref, rank_ref, e2_ref, e1_ref, cnt_ref, x_ref, g_ref,
                       b_ref, o_ref, acc_ref, sc_ref, p_ref, *, n_chunks):
    j = pl.program_id(1)
    tb = p_ref.shape[1]

    @pl.when(j == 0)
    def _():
        acc_ref[...] = jnp.zeros_like(acc_ref)

    xt = xt_ref[...]
    n_sub = PEER_CHUNK // PEER_SUB
    subs = [slice(c * PEER_SUB, (c + 1) * PEER_SUB) for c in range(n_sub)]
    sc_ref[subs[0], :] = jnp.dot(u_ref[subs[0], :], xt, preferred_element_type=F32)
    for c in range(n_sub):
        if c + 1 < n_sub:
            sc_ref[subs[c + 1], :] = jnp.dot(u_ref[subs[c + 1], :], xt, preferred_element_type=F32)
        for r in range(c * (PEER_SUB // N_KEYS), (c + 1) * (PEER_SUB // N_KEYS)):
            for i2h, lg in itertools.product(range(N_KEYS // GATE_ROWS), range(tb // LANES)):
                cols = slice(lg * LANES, (lg + 1) * LANES)
                rows = slice(r * N_KEYS + i2h * GATE_ROWS, r * N_KEYS + (i2h + 1) * GATE_ROWS)
                w = None
                for h in range(PEER_HEADS):
                    hr = slice(h * N_KEYS + i2h * GATE_ROWS, h * N_KEYS + (i2h + 1) * GATE_ROWS)
                    cnt = cnt_ref[h, r:r + 1, cols].astype(BF16)
                    e1 = e1_ref[h, r:r + 1, cols].astype(BF16)
                    gate = jnp.where(rank_ref[hr, cols] < cnt, e2_ref[hr, cols] * e1,
                                     jnp.zeros((), BF16))
                    w = gate if w is None else w + gate
                p_ref[rows, cols] = w * _gelu(sc_ref[rows, cols]).astype(BF16)
        acc_ref[...] += jnp.dot(vt_ref[0, :, subs[c]], p_ref[subs[c], :],
                                preferred_element_type=F32)

    @pl.when(j == n_chunks - 1)
    def _():
        o_ref[...] = _layer_norm(ALPHA * x_ref[...] + acc_ref[...].T, g_ref[...], b_ref[...])


def _peer_dense(x1t_bf, u_bf, vt_bf, e1t, e2t, cntt, rankt, x1, g, b, *, tb):
    n = x1.shape[0]
    e = PEER_CHUNK
    n_chunks = N_EXPERTS // e
    rpc = e // N_KEYS
    half = PEER_HEADS * N_KEYS
    tok = pl.BlockSpec((half, tb), lambda i, j: (0, i))
    by_head = lambda a: a.reshape(PEER_HEADS, N_KEYS, n)
    rowspec = pl.BlockSpec((PEER_HEADS, rpc, tb), lambda i, j: (0, j, i))
    return pl.pallas_call(
        functools.partial(_peer_dense_kernel, n_chunks=n_chunks),
        grid=(n // tb, n_chunks),
        in_specs=[
            pl.BlockSpec((D_MODEL, tb), lambda i, j: (0, i)),
            pl.BlockSpec((e, D_MODEL), lambda i, j: (j, 0)),
            pl.BlockSpec((1, D_MODEL, e), lambda i, j: (j, 0, 0)),
            tok, tok,
            rowspec, rowspec,
            pl.BlockSpec((tb, D_MODEL), lambda i, j: (i, 0), pipeline_mode=pl.Buffered(1)),
            pl.BlockSpec((1, D_MODEL), lambda i, j: (0, 0)),
            pl.BlockSpec((1, D_MODEL), lambda i, j: (0, 0)),
        ],
        out_specs=pl.BlockSpec((tb, D_MODEL), lambda i, j: (i, 0)),
        out_shape=jax.ShapeDtypeStruct((n, D_MODEL), F32),
        scratch_shapes=[
            pltpu.VMEM((D_MODEL, tb), F32),
            pltpu.VMEM((e, tb), F32),
            pltpu.VMEM((e, tb), BF16),
        ],
        compiler_params=_cparams(("parallel", "arbitrary")),
        name="peer_dense",
    )(x1t_bf, u_bf, vt_bf, rankt, e2t, by_head(e1t), by_head(cntt), x1, g, b)


def _rope_tables(pos):
    half = QK_DIM // 2
    inv = ROPE_THETA ** (-jnp.arange(half, dtype=F32) * 2.0 / QK_DIM)
    ang = pos.astype(F32)[:, None] * inv[None, :]
    cos, sin = jnp.cos(ang), jnp.sin(ang)
    reps = LANES // QK_DIM
    return (jnp.tile(jnp.concatenate([cos, cos], axis=1), (1, reps)),
            jnp.tile(jnp.concatenate([-sin, sin], axis=1), (1, reps)))


def _peer_mixer(x1, x1_bf, w, *, tm, tl, tb):
    st = _peer_scores(x1_bf, w["wq"], w["k1"], w["k2"], tm=tm)
    e1t, e2t, cntt, rankt = _peer_stats(st, tl=tl)
    return _peer_dense(x1_bf.T, w["u"], w["vt"], e1t, e2t, cntt, rankt, x1, w["ln2_g"], w["ln2_b"],
                       tb=tb)


def kernel(x_prompt, x_sample, cache_k, cache_v, state_conv, state_h, page_table, w_in, conv_w, conv_b, lru_wa, lru_ba, lru_wx, lru_bx, lru_lambda, lambda_q1, lambda_k1, lambda_q2, lambda_k2, subln_g, w_out, ln1_g, ln1_b, peer_wq, peer_k1, peer_k2, peer_u, peer_v, ln2_g, ln2_b):
    bsz, seq, _ = x_prompt.shape
    dbsz, dseq, _ = x_sample.shape
    n_pages = page_table.shape[1]
    page = cache_k.shape[2]
    past = n_pages * page
    n = bsz * seq
    l = 0

    w_in_bf = w_in[l].astype(BF16)
    w_rec = w_in_bf[:, :2 * D_REC]
    w_q = w_in_bf[:, 2 * D_REC:2 * D_REC + D_ATT]
    w_k = w_in_bf[:, 2 * D_REC + D_ATT:2 * D_REC + 2 * D_ATT]
    w_v = w_in_bf[:, 2 * D_REC + 2 * D_ATT:]
    w_out_bf = w_out[l].astype(BF16)
    wa_bf, wx_bf = lru_wa[l].astype(BF16), lru_wx[l].astype(BF16)
    row = lambda a: a[l].reshape(1, -1)
    lam_p = jnp.stack([lambda_q1[l], lambda_k1[l], lambda_q2[l], lambda_k2[l]])
    peer_w = dict(wq=peer_wq[l].astype(BF16), k1=peer_k1[l].astype(BF16), k2=peer_k2[l].astype(BF16),
                  u=peer_u[l].astype(BF16),
                  vt=peer_v[l].reshape(N_EXPERTS // PEER_CHUNK, PEER_CHUNK, D_MODEL)
                  .transpose(0, 2, 1).astype(BF16),
                  ln2_g=row(ln2_g), ln2_b=row(ln2_b))

    xp = x_prompt.reshape(n, D_MODEL)
    xp_bf = xp.astype(BF16)
    cos_p, sin_p = _rope_tables(jnp.arange(seq, dtype=jnp.int32))
    proj = functools.partial(_proj, xp_bf, cos_t=cos_p, sin_t=sin_p, tm=1024, tn=512)
    rec_p = proj(w_rec, rope=False, out_dtype=F32)
    q_p = proj(w_q, rope=True, out_dtype=BF16)
    k_p = proj(w_k, rope=True, out_dtype=F32)
    v_p = proj(w_v, rope=False, out_dtype=F32)
    rec_out_p, h_p = _rec_prompt(rec_p, conv_w[l], row(conv_b), wa_bf, wx_bf, row(lru_ba),
                                 row(lru_bx), row(lru_lambda), bsz=bsz, t=seq, cb=256)
    att_p = _attn_prompt(q_p, k_p, v_p, lam_p, row(subln_g), bsz=bsz, t=seq, tq=512)
    x1_p, x1_p_bf = _outproj(rec_out_p, att_p, w_out_bf[:D_REC], w_out_bf[D_REC:], xp,
                             row(ln1_g), row(ln1_b), tm=256)
    y_p = _peer_mixer(x1_p, x1_p_bf, peer_w, tm=256, tl=256, tb=512)

    xs = x_sample.reshape(dbsz * dseq, D_MODEL)
    ns = xs.shape[0]
    xs_bf = xs.astype(BF16)
    cos_s, sin_s = _rope_tables(jnp.full((ns,), past, dtype=jnp.int32))
    proj_s = functools.partial(_proj, xs_bf, cos_t=cos_s, sin_t=sin_s, tm=ns, tn=512)
    rec_s = proj_s(w_rec, rope=False, out_dtype=F32)
    q_s = proj_s(w_q, rope=True, out_dtype=F32)
    k_s = proj_s(w_k, rope=True, out_dtype=F32)
    v_s = proj_s(w_v, rope=False, out_dtype=F32)
    rec_out_s, conv_s_t, h_s = _rec_sample(
        rec_s, state_conv[l].transpose(1, 0, 2), state_h[l], conv_w[l], row(conv_b), wa_bf, wx_bf,
        row(lru_ba), row(lru_bx), row(lru_lambda))
    n_pool = cache_k.shape[1]
    att_s = _attn_sample(
        page_table, q_s.reshape(ns, N_SLOTS, QK_DIM), k_s.reshape(ns, N_SLOTS, QK_DIM),
        jnp.repeat(v_s.reshape(ns, N_ATT_HEADS, ATT_HEAD_DIM), 2, axis=1),
        cache_k[l].transpose(0, 2, 3, 1),
        cache_v[l].reshape(n_pool, page * N_ATT_HEADS, ATT_HEAD_DIM), lam_p, row(subln_g))
    pad = LANES - ns
    padr = lambda a: jnp.pad(a, ((0, pad), (0, 0)))
    x1_s, x1_s_bf = _outproj(padr(rec_out_s), padr(att_s.reshape(ns, D_ATT).astype(BF16)), w_out_bf[:D_REC],
                             w_out_bf[D_REC:], padr(xs), row(ln1_g), row(ln1_b), tm=LANES)
    y_s = _peer_mixer(x1_s, x1_s_bf, peer_w, tm=LANES, tl=LANES, tb=LANES)[:ns]

    k4 = (2 * N_ATT_HEADS, QK_DIM)
    v4 = (N_ATT_HEADS, ATT_HEAD_DIM)
    return (
        y_p.reshape(bsz, seq, D_MODEL),
        y_s.reshape(dbsz, dseq, D_MODEL),
        k_p.reshape(1, bsz, seq, *k4),
        v_p.reshape(1, bsz, seq, *v4),
        rec_p.reshape(bsz, seq, 2 * D_REC)[:, seq - (CONV_W - 1):, :D_REC][None],
        h_p.reshape(1, bsz, D_REC),
        k_s.reshape(1, dbsz, dseq, *k4),
        v_s.reshape(1, dbsz, dseq, *v4),
        conv_s_t.transpose(1, 0, 2)[None],
        h_s.reshape(1, dbsz, D_REC),
    )
```

```python
import functools
import itertools
import math

import jax
import jax.numpy as jnp
from jax import lax
from jax.experimental import pallas as pl
from jax.experimental.pallas import tpu as pltpu

F32 = jnp.float32
BF16 = jnp.bfloat16

D_MODEL = 2048
D_REC = 1024
D_ATT = 1024
N_REC_BLOCKS = 8
REC_BLOCK = 128
CONV_W = 4
LRU_C = 8.0
N_ATT_HEADS = 8
ATT_HEAD_DIM = 128
QK_DIM = 64
ROPE_THETA = 10000.0
N_KEYS = 128
N_EXPERTS = N_KEYS * N_KEYS
PEER_HEADS = 8
PEER_TOPK = 16
LN_EPS = 1e-5
NEG_INF = -1e30
DEPTH = 1
ALPHA = (2.0 * DEPTH) ** 0.25
LAM_INIT = 0.8 - 0.6 * math.exp(-0.3 * 0)
LANES = 128

VMEM_LIMIT = 56 * 1024 * 1024


def _cparams(sem, flags=None):
    return pltpu.CompilerParams(dimension_semantics=sem, vmem_limit_bytes=VMEM_LIMIT, flags=flags)


def _gelu(x):
    c = math.sqrt(2.0 / math.pi)
    return x * (0.5 * (1.0 + jnp.tanh(c * (x + 0.044715 * (x * x * x)))))


def _layer_norm(y, g, b):
    mu = jnp.mean(y, axis=-1, keepdims=True)
    yc = y - mu
    var = jnp.mean(yc * yc, axis=-1, keepdims=True)
    return yc * lax.rsqrt(var + LN_EPS) * g + b


def _proj_kernel(x_ref, w_ref, cos_ref, sin_ref, o_ref, *, rope):
    acc = jnp.dot(x_ref[...], w_ref[...], preferred_element_type=F32)
    if rope:
        c = cos_ref[...]
        s = sin_ref[...]
        lane = lax.broadcasted_iota(jnp.int32, (1, LANES), 1)
        first_half = (lane % QK_DIM) < (QK_DIM // 2)
        outs = []
        for j in range(acc.shape[1] // LANES):
            xc = acc[:, j * LANES:(j + 1) * LANES]
            fwd = pltpu.roll(xc, LANES - QK_DIM // 2, axis=1)
            bwd = pltpu.roll(xc, QK_DIM // 2, axis=1)
            partner = jnp.where(first_half, fwd, bwd)
            outs.append(xc * c + partner * s)
        acc = jnp.concatenate(outs, axis=1)
    o_ref[...] = acc.astype(o_ref.dtype)


def _proj(x_bf, w_bf, cos_t, sin_t, *, rope, out_dtype, tm, tn):
    n, k = x_bf.shape
    m = w_bf.shape[1]
    n_pos_blocks = cos_t.shape[0] // tm
    return pl.pallas_call(
        functools.partial(_proj_kernel, rope=rope),
        grid=(n // tm, m // tn),
        in_specs=[
            pl.BlockSpec((tm, k), lambda i, j: (i, 0)),
            pl.BlockSpec((k, tn), lambda i, j: (0, j)),
            pl.BlockSpec((tm, LANES), lambda i, j: (i % n_pos_blocks, 0)),
            pl.BlockSpec((tm, LANES), lambda i, j: (i % n_pos_blocks, 0)),
        ],
        out_specs=pl.BlockSpec((tm, tn), lambda i, j: (i, j)),
        out_shape=jax.ShapeDtypeStruct((n, m), out_dtype),
        compiler_params=_cparams(("parallel", "parallel")),
        name="in_proj_rope" if rope else "in_proj",
    )(x_bf, w_bf, cos_t, sin_t)


def _lru_gates(conv, wa_ref, wx_ref, ba, bx, lam):
    rs, is_ = [], []
    for blk in range(conv.shape[1] // REC_BLOCK):
        cb = conv[:, blk * REC_BLOCK:(blk + 1) * REC_BLOCK].astype(BF16)
        rs.append(jnp.dot(cb, wa_ref[blk], preferred_element_type=F32))
        is_.append(jnp.dot(cb, wx_ref[blk], preferred_element_type=F32))
    r = jax.nn.sigmoid(jnp.concatenate(rs, axis=1) + ba)
    i = jax.nn.sigmoid(jnp.concatenate(is_, axis=1) + bx)
    softplus_neg_lam = jnp.maximum(-lam, 0.0) + jnp.log1p(jnp.exp(-jnp.abs(lam)))
    log_a = -LRU_C * r * softplus_neg_lam
    a = jnp.exp(log_a)
    b = jnp.sqrt(1.0 - jnp.exp(2.0 * log_a)) * (i * conv)
    return a, b


def _shift_rows(x, s, fill):
    row = lax.broadcasted_iota(jnp.int32, x.shape, 0)
    return jnp.where(row >= s, pltpu.roll(x, s, axis=0), fill)


SCAN_GROUP = 8


def _rec_prompt_kernel(x_ref, g_ref, cw_ref, cb_ref, wa_ref, wx_ref, ba_ref, bx_ref, lam_ref,
                       o_ref, h_ref, a_sc, b_sc):
    x = x_ref[...]
    t = x.shape[0]
    cw = cw_ref[...]
    conv = (cb_ref[...] + cw[3:4] * x + cw[2:3] * _shift_rows(x, 1, 0.0)
            + cw[1:2] * _shift_rows(x, 2, 0.0) + cw[0:1] * _shift_rows(x, 3, 0.0))
    a, b = _lru_gates(conv, wa_ref, wx_ref, ba_ref[...], bx_ref[...], lam_ref[...])
    in_tile = lax.broadcasted_iota(jnp.int32, a.shape, 0) % SCAN_GROUP
    s = 1
    while s < SCAN_GROUP:
        b = a * jnp.where(in_tile >= s, pltpu.roll(b, s, axis=0), 0.0) + b
        a = a * jnp.where(in_tile >= s, pltpu.roll(a, s, axis=0), 1.0)
        s *= 2
    a_sc[...] = a
    b_sc[...] = b

    def tile(g, carry):
        rows = pl.ds(pl.multiple_of(g * SCAN_GROUP, SCAN_GROUP), SCAN_GROUP)
        h = a_sc[rows, :] * carry + b_sc[rows, :]
        b_sc[rows, :] = h
        return h[SCAN_GROUP - 1:SCAN_GROUP, :]

    h_last = lax.fori_loop(0, t // SCAN_GROUP, tile, jnp.zeros((1, x.shape[1]), F32), unroll=8)
    o_ref[...] = (b_sc[...] * _gelu(g_ref[...])).astype(o_ref.dtype)
    h_ref[0] = h_last


def _rec_prompt(rec, conv_w, conv_b, wa_bf, wx_bf, ba, bx, lam, *, bsz, t, cb):
    ncb = D_REC // cb
    return pl.pallas_call(
        _rec_prompt_kernel,
        grid=(bsz, ncb),
        in_specs=[
            pl.BlockSpec((t, cb), lambda b, c: (b, c)),
            pl.BlockSpec((t, cb), lambda b, c: (b, ncb + c)),
            pl.BlockSpec((CONV_W, cb), lambda b, c: (0, c)),
            pl.BlockSpec((1, cb), lambda b, c: (0, c)),
            pl.BlockSpec((cb // REC_BLOCK, REC_BLOCK, REC_BLOCK), lambda b, c: (c, 0, 0)),
            pl.BlockSpec((cb // REC_BLOCK, REC_BLOCK, REC_BLOCK), lambda b, c: (c, 0, 0)),
            pl.BlockSpec((1, cb), lambda b, c: (0, c)),
            pl.BlockSpec((1, cb), lambda b, c: (0, c)),
            pl.BlockSpec((1, cb), lambda b, c: (0, c)),
        ],
        out_specs=[
            pl.BlockSpec((t, cb), lambda b, c: (b, c)),
            pl.BlockSpec((1, 1, cb), lambda b, c: (b, 0, c)),
        ],
        out_shape=[
            jax.ShapeDtypeStruct((bsz * t, D_REC), BF16),
            jax.ShapeDtypeStruct((bsz, 1, D_REC), F32),
        ],
        scratch_shapes=[pltpu.VMEM((t, cb), F32), pltpu.VMEM((t, cb), F32)],
        compiler_params=_cparams(("parallel", "parallel")),
        name="rglru_prompt",
    )(rec, rec, conv_w, conv_b, wa_bf, wx_bf, ba, bx, lam)


def _rec_sample_kernel(rec_ref, sc_ref, h0_ref, cw_ref, cb_ref, wa_ref, wx_ref, ba_ref, bx_ref,
                       lam_ref, o_ref, nc_ref, h_ref):
    x = rec_ref[:, :D_REC]
    g = rec_ref[:, D_REC:]
    cw = cw_ref[...]
    conv = (cb_ref[...] + cw[0:1] * sc_ref[0] + cw[1:2] * sc_ref[1] + cw[2:3] * sc_ref[2]
            + cw[3:4] * x)
    a, b = _lru_gates(conv, wa_ref, wx_ref, ba_ref[...], bx_ref[...], lam_ref[...])
    h = a * h0_ref[...] + b
    o_ref[...] = (h * _gelu(g)).astype(o_ref.dtype)
    nc_ref[0] = sc_ref[1]
    nc_ref[1] = sc_ref[2]
    nc_ref[2] = x
    h_ref[...] = h


def _rec_sample(rec, sc_t, h0, conv_w, conv_b, wa_bf, wx_bf, ba, bx, lam):
    n = rec.shape[0]
    return pl.pallas_call(
        _rec_sample_kernel,
        out_shape=[
            jax.ShapeDtypeStruct((n, D_REC), BF16),
            jax.ShapeDtypeStruct((CONV_W - 1, n, D_REC), F32),
            jax.ShapeDtypeStruct((n, D_REC), F32),
        ],
        compiler_params=pltpu.CompilerParams(vmem_limit_bytes=VMEM_LIMIT),
        name="rglru_sample",
    )(rec, sc_t, h0, conv_w, conv_b, wa_bf, wx_bf, ba, bx, lam)


def _diff_lambda(lp):
    d1 = jnp.sum(lp[0:1] * lp[1:2], axis=1, keepdims=True)
    d2 = jnp.sum(lp[2:3] * lp[3:4], axis=1, keepdims=True)
    return jnp.exp(d1) - jnp.exp(d2) + LAM_INIT


def _nt_dot(a, b):
    return lax.dot_general(a, b, (((1,), (1,)), ((), ())), preferred_element_type=F32)


def _attn_prompt_kernel(q_ref, k_ref, v_ref, lp_ref, g_ref, o_ref, m_sc, l_sc, acc_sc, *, tq):
    qi = pl.program_id(2)
    lane = lax.broadcasted_iota(jnp.int32, (1, LANES), 1)
    qs = q_ref[...] * (QK_DIM ** -0.5)
    zero = jnp.zeros_like(qs)
    qmaps = (jnp.where(lane < QK_DIM, qs, zero), jnp.where(lane >= QK_DIM, qs, zero))
    m_sc[...] = jnp.full(m_sc.shape, NEG_INF, F32)
    l_sc[...] = jnp.zeros_like(l_sc)
    acc_sc[...] = jnp.zeros_like(acc_sc)
    reps = tq // LANES

    def block(kb, masked):
        start = pl.multiple_of(kb * tq, tq)
        k = k_ref[pl.ds(start, tq), :].astype(BF16)
        v = v_ref[pl.ds(start, tq), :].astype(BF16)
        for c in range(2):
            s = _nt_dot(qmaps[c], k)
            if masked:
                row = lax.broadcasted_iota(jnp.int32, s.shape, 0)
                col = lax.broadcasted_iota(jnp.int32, s.shape, 1)
                s = jnp.where(col <= row, s, NEG_INF)
            m_old = m_sc[c]
            m_new = jnp.maximum(m_old, jnp.max(s, axis=1, keepdims=True))
            alpha = jnp.exp(m_old - m_new)
            p = jnp.exp(s - jnp.tile(m_new, (1, reps)))
            l_sc[c] = alpha * l_sc[c] + jnp.sum(p, axis=1, keepdims=True)
            acc_sc[c] = alpha * acc_sc[c] + jnp.dot(p.astype(BF16), v, preferred_element_type=F32)
            m_sc[c] = m_new

    @pl.loop(0, qi)
    def _(kb):
        block(kb, False)

    block(qi, True)
    lam = _diff_lambda(lp_ref[...])
    att = acc_sc[0] / l_sc[0] - lam * (acc_sc[1] / l_sc[1])
    ms = jnp.mean(att * att, axis=1, keepdims=True)
    att = att * lax.rsqrt(ms + LN_EPS) * g_ref[...] * (1.0 - LAM_INIT)
    o_ref[...] = att.astype(o_ref.dtype)


def _attn_prompt(q_bf, k, v, lam_p, subln_g, *, bsz, t, tq):
    nq = t // tq
    return pl.pallas_call(
        functools.partial(_attn_prompt_kernel, tq=tq),
        grid=(bsz, N_ATT_HEADS, nq),
        in_specs=[
            pl.BlockSpec((tq, LANES), lambda b, h, i: (b * nq + i, h)),
            pl.BlockSpec((t, LANES), lambda b, h, i: (b, h)),
            pl.BlockSpec((t, LANES), lambda b, h, i: (b, h)),
            pl.BlockSpec((4, QK_DIM), lambda b, h, i: (0, 0)),
            pl.BlockSpec((1, ATT_HEAD_DIM), lambda b, h, i: (0, 0)),
        ],
        out_specs=pl.BlockSpec((tq, LANES), lambda b, h, i: (b * nq + i, h)),
        out_shape=jax.ShapeDtypeStruct((bsz * t, D_ATT), BF16),
        scratch_shapes=[
            pltpu.VMEM((2, tq, LANES), F32),
            pltpu.VMEM((2, tq, LANES), F32),
            pltpu.VMEM((2, tq, ATT_HEAD_DIM), F32),
        ],
        compiler_params=_cparams(("parallel", "parallel", "arbitrary")),
        name="diff_attn_prompt",
    )(q_bf, k, v, lam_p, subln_g)


N_SLOTS = 2 * N_ATT_HEADS
PAGES_PER_STEP = 8


def _attn_sample_kernel(pt_ref, q_ref, qb_ref, kn_ref, vn_ref, *refs, n_steps, pps, page):
    del pt_ref
    k_refs, v_refs = refs[:pps], refs[pps:2 * pps]
    lp_ref, g_ref, o_ref, m_sc, l_sc, acc_sc, o_sc = refs[2 * pps:]
    step = pl.program_id(1)
    q = q_ref[0] * (QK_DIM ** -0.5)

    @pl.when(step == 0)
    def _():
        s_self = jnp.sum(q * kn_ref[0], axis=1, keepdims=True)
        m_sc[...] = jnp.broadcast_to(s_self, m_sc.shape)
        l_sc[...] = jnp.ones_like(l_sc)
        acc_sc[...] = vn_ref[0]

    qb = qb_ref[0]
    slot_head = lax.broadcasted_iota(jnp.int32, (N_SLOTS, LANES), 0) // 2
    for i in range(pps):
        st = jnp.sum(k_refs[i][0] * qb, axis=1) * (QK_DIM ** -0.5)
        m_old = m_sc[...]
        m_new = jnp.maximum(m_old, jnp.max(st, axis=1, keepdims=True))
        alpha = jnp.exp(m_old - m_new)
        p = jnp.exp(st - m_new)
        l_sc[...] = alpha * l_sc[...] + jnp.sum(p, axis=1, keepdims=True)
        m_sc[...] = m_new
        p_bf = p.astype(BF16)
        pv = jnp.zeros((N_SLOTS, ATT_HEAD_DIM), F32)
        for h in range(N_ATT_HEADS):
            v_h = v_refs[i][0, pl.ds(h, page, stride=N_ATT_HEADS), :].astype(BF16)
            full = jnp.dot(p_bf, v_h, preferred_element_type=F32)
            pv = jnp.where(slot_head == h, full, pv)
        acc_sc[...] = alpha * acc_sc[...] + pv

    @pl.when(step == n_steps - 1)
    def _():
        lam = _diff_lambda(lp_ref[...])
        o_sc[...] = acc_sc[...] / l_sc[...]
        o1 = o_sc[pl.ds(0, N_ATT_HEADS, stride=2), :]
        o2 = o_sc[pl.ds(1, N_ATT_HEADS, stride=2), :]
        att = o1 - lam * o2
        ms = jnp.mean(att * att, axis=1, keepdims=True)
        o_ref[0] = att * lax.rsqrt(ms + LN_EPS) * g_ref[...] * (1.0 - LAM_INIT)


def _attn_sample(page_table, q, k_new, v_new2, cache_kt, cache_v2, lam_p, subln_g):
    nb, n_pages = page_table.shape
    pps = PAGES_PER_STEP
    page = cache_kt.shape[3]
    qb = jnp.broadcast_to(q[..., None], (*q.shape, page))
    n_steps = n_pages // pps
    kspec = lambda i: pl.BlockSpec((1, N_SLOTS, QK_DIM, page),
                                   lambda b, s, pt: (pt[b, s * pps + i], 0, 0, 0))
    vspec = lambda i: pl.BlockSpec((1, page * N_ATT_HEADS, ATT_HEAD_DIM),
                                   lambda b, s, pt: (pt[b, s * pps + i], 0, 0))
    grid_spec = pltpu.PrefetchScalarGridSpec(
        num_scalar_prefetch=1,
        grid=(nb, n_steps),
        in_specs=[
            pl.BlockSpec((1, N_SLOTS, QK_DIM), lambda b, s, pt: (b, 0, 0)),
            pl.BlockSpec((1, N_SLOTS, QK_DIM, page), lambda b, s, pt: (b, 0, 0, 0)),
            pl.BlockSpec((1, N_SLOTS, QK_DIM), lambda b, s, pt: (b, 0, 0)),
            pl.BlockSpec((1, N_SLOTS, ATT_HEAD_DIM), lambda b, s, pt: (b, 0, 0)),
            *[kspec(i) for i in range(pps)],
            *[vspec(i) for i in range(pps)],
            pl.BlockSpec((4, QK_DIM), lambda b, s, pt: (0, 0)),
            pl.BlockSpec((1, ATT_HEAD_DIM), lambda b, s, pt: (0, 0)),
        ],
        out_specs=pl.BlockSpec((1, N_ATT_HEADS, ATT_HEAD_DIM), lambda b, s, pt: (b, 0, 0)),
        scratch_shapes=[
            pltpu.VMEM((N_SLOTS, LANES), F32),
            pltpu.VMEM((N_SLOTS, LANES), F32),
            pltpu.VMEM((N_SLOTS, ATT_HEAD_DIM), F32),
            pltpu.VMEM((N_SLOTS, ATT_HEAD_DIM), F32),
        ],
    )
    return pl.pallas_call(
        functools.partial(_attn_sample_kernel, n_steps=n_steps, pps=pps, page=page),
        grid_spec=grid_spec,
        out_shape=jax.ShapeDtypeStruct((nb, N_ATT_HEADS, ATT_HEAD_DIM), F32),
        compiler_params=_cparams(("parallel", "arbitrary")),
        name="diff_attn_sample",
    )(page_table, q, qb, k_new, v_new2, *([cache_kt] * pps), *([cache_v2] * pps), lam_p, subln_g)


def _outproj_kernel(rec_ref, att_ref, wr_ref, wa_ref, x_ref, g_ref, b_ref, o_ref, ob_ref):
    mix = jnp.dot(rec_ref[...], wr_ref[...], preferred_element_type=F32)
    mix = mix + jnp.dot(att_ref[...], wa_ref[...], preferred_element_type=F32)
    y = _layer_norm(ALPHA * x_ref[...] + mix, g_ref[...], b_ref[...])
    o_ref[...] = y
    ob_ref[...] = y.astype(BF16)


def _outproj(rec_o, att_o, w_rec, w_att, x, g, b, *, tm):
    n = x.shape[0]
    return pl.pallas_call(
        _outproj_kernel,
        grid=(n // tm,),
        in_specs=[
            pl.BlockSpec((tm, D_REC), lambda i: (i, 0)),
            pl.BlockSpec((tm, D_ATT), lambda i: (i, 0)),
            pl.BlockSpec((D_REC, D_MODEL), lambda i: (0, 0)),
            pl.BlockSpec((D_ATT, D_MODEL), lambda i: (0, 0)),
            pl.BlockSpec((tm, D_MODEL), lambda i: (i, 0)),
            pl.BlockSpec((1, D_MODEL), lambda i: (0, 0)),
            pl.BlockSpec((1, D_MODEL), lambda i: (0, 0)),
        ],
        out_specs=[
            pl.BlockSpec((tm, D_MODEL), lambda i: (i, 0)),
            pl.BlockSpec((tm, D_MODEL), lambda i: (i, 0)),
        ],
        out_shape=[
            jax.ShapeDtypeStruct((n, D_MODEL), F32),
            jax.ShapeDtypeStruct((n, D_MODEL), BF16),
        ],
        compiler_params=_cparams(("parallel",)),
        name="out_proj_ln",
    )(rec_o, att_o, w_rec, w_att, x, g, b)


def _peer_scores_kernel(x_ref, wq_ref, k1_ref, k2_ref, o_ref):
    q = jnp.dot(x_ref[...], wq_ref[...], preferred_element_type=F32).astype(BF16)
    for h in range(PEER_HEADS):
        for c, k_ref in enumerate((k1_ref, k2_ref)):
            qb = q[:, (2 * h + c) * N_KEYS:(2 * h + c + 1) * N_KEYS]
            base = (c * PEER_HEADS + h) * N_KEYS
            o_ref[base:base + N_KEYS, :] = _nt_dot(k_ref[h], qb)


def _peer_scores(x1_bf, wq_bf, k1_bf, k2_bf, *, tm):
    n = x1_bf.shape[0]
    rows = 2 * PEER_HEADS * N_KEYS
    return pl.pallas_call(
        _peer_scores_kernel,
        grid=(n // tm,),
        in_specs=[
            pl.BlockSpec((tm, D_MODEL), lambda i: (i, 0)),
            pl.BlockSpec((D_MODEL, rows), lambda i: (0, 0)),
            pl.BlockSpec((PEER_HEADS, N_KEYS, N_KEYS), lambda i: (0, 0, 0)),
            pl.BlockSpec((PEER_HEADS, N_KEYS, N_KEYS), lambda i: (0, 0, 0)),
        ],
        out_specs=pl.BlockSpec((rows, tm), lambda i: (0, i)),
        out_shape=jax.ShapeDtypeStruct((rows, n), F32),
        compiler_params=_cparams(("parallel",)),
        name="peer_scores",
    )(x1_bf, wq_bf, k1_bf, k2_bf)


def _top_values(x, k):
    vals = []
    for _ in range(k):
        cur = jnp.max(x, axis=0, keepdims=True)
        vals.append(cur)
        x = jnp.where(x == cur, -jnp.inf, x)
    return jnp.concatenate(vals, axis=0)


def _peer_stats_kernel(s_ref, e1_ref, e2_ref, cnt_ref, rank_ref):
    half = PEER_HEADS * N_KEYS
    k = PEER_TOPK
    for h in range(PEER_HEADS):
        r1 = slice(h * N_KEYS, (h + 1) * N_KEYS)
        s1 = s_ref[r1, :]
        s2 = s_ref[half + h * N_KEYS:half + (h + 1) * N_KEYS, :]
        v1 = _top_values(s1, k)
        v2 = _top_values(s2, k)
        sub = lax.broadcasted_iota(jnp.int32, (8, 1), 0)
        cands = [v1[0:8] + v2[0:1], v1[8:16] + v2[0:1], v2[8:16] + v1[0:1]]
        for b in range(1, 8):
            n_a = k // (b + 1)
            cands.append(jnp.where(sub < n_a, v1[0:8] + v2[b:b + 1], -jnp.inf))
        cand = jnp.concatenate(cands, axis=0)
        work = cand
        theta = None
        for _ in range(k):
            theta = jnp.max(work, axis=0, keepdims=True)
            work = jnp.where(work == theta, -jnp.inf, work)
        m = v1[0:1] + v2[0:1]
        z = jnp.sum(jnp.where(cand >= theta, jnp.exp(cand - m), 0.0), axis=0, keepdims=True)
        e1_ref[r1, :] = jnp.exp(s1 - (v1[0:1] + jnp.log(z)))
        e2_ref[r1, :] = jnp.exp(s2 - v2[0:1]).astype(BF16)
        cnt = jnp.zeros(s1.shape, F32)
        rank = jnp.zeros(s2.shape, F32)
        for b in range(k):
            cnt = cnt + jnp.where(s1 + v2[b:b + 1] >= theta, 1.0, 0.0)
            rank = rank + jnp.where(v2[b:b + 1] > s2, 1.0, 0.0)
        cnt_ref[r1, :] = cnt
        rank_ref[r1, :] = rank.astype(BF16)


def _peer_stats(st, *, tl):
    rows, n = st.shape
    half = rows // 2
    spec = pl.BlockSpec((half, tl), lambda i: (0, i))
    f32, bf16 = jax.ShapeDtypeStruct((half, n), F32), jax.ShapeDtypeStruct((half, n), BF16)
    return pl.pallas_call(
        _peer_stats_kernel,
        grid=(n // tl,),
        in_specs=[pl.BlockSpec((rows, tl), lambda i: (0, i))],
        out_specs=[spec] * 4,
        out_shape=[f32, bf16, f32, bf16],
        compiler_params=_cparams(("parallel",)),
        name="peer_stats",
    )(st)


PEER_CHUNK = 8 * N_KEYS
PEER_SUB = 4 * N_KEYS
GATE_ROWS = 64


def _peer_dense_kernel(xt_ref, u_ref, vt_ref, rank_ref, e2_ref, e1_ref, cnt_ref, x_ref, g_ref,
                       b_ref, o_ref, acc_ref, sc_ref, p_ref, *, n_chunks):
    j = pl.program_id(1)
    tb = p_ref.shape[1]

    @pl.when(j == 0)
    def _():
        acc_ref[...] = jnp.zeros_like(acc_ref)

    xt = xt_ref[...]
    n_sub = PEER_CHUNK // PEER_SUB
    subs = [slice(c * PEER_SUB, (c + 1) * PEER_SUB) for c in range(n_sub)]
    sc_ref[subs[0], :] = jnp.dot(u_ref[subs[0], :], xt, preferred_element_type=F32)
    for c in range(n_sub):
        if c + 1 < n_sub:
            sc_ref[subs[c + 1], :] = jnp.dot(u_ref[subs[c + 1], :], xt, preferred_element_type=F32)
        for r in range(c * (PEER_SUB // N_KEYS), (c + 1) * (PEER_SUB // N_KEYS)):
            for i2h, lg in itertools.product(range(N_KEYS // GATE_ROWS), range(tb // LANES)):
                cols = slice(lg * LANES, (lg + 1) * LANES)
                rows = slice(r * N_KEYS + i2h * GATE_ROWS, r * N_KEYS + (i2h + 1) * GATE_ROWS)
                w = None
                for h in range(PEER_HEADS):
                    hr = slice(h * N_KEYS + i2h * GATE_ROWS, h * N_KEYS + (i2h + 1) * GATE_ROWS)
                    cnt = cnt_ref[h, r:r + 1, cols].astype(BF16)
                    e1 = e1_ref[h, r:r + 1, cols].astype(BF16)
                    gate = jnp.where(rank_ref[hr, cols] < cnt, e2_ref[hr, cols] * e1,
                                     jnp.zeros((), BF16))
                    w = gate if w is None else w + gate
                p_ref[rows, cols] = w * _gelu(sc_ref[rows, cols]).astype(BF16)
        acc_ref[...] += jnp.dot(vt_ref[0, :, subs[c]], p_ref[subs[c], :],
                                preferred_element_type=F32)

    @pl.when(j == n_chunks - 1)
    def _():
        o_ref[...] = _layer_norm(ALPHA * x_ref[...] + acc_ref[...].T, g_ref[...], b_ref[...])


def _peer_dense(x1t_bf, u_bf, vt_bf, e1t, e2t, cntt, rankt, x1, g, b, *, tb):
    n = x1.shape[0]
    e = PEER_CHUNK
    n_chunks = N_EXPERTS // e
    rpc = e // N_KEYS
    half = PEER_HEADS * N_KEYS
    tok = pl.BlockSpec((half, tb), lambda i, j: (0, i))
    by_head = lambda a: a.reshape(PEER_HEADS, N_KEYS, n)
    rowspec = pl.BlockSpec((PEER_HEADS, rpc, tb), lambda i, j: (0, j, i))
    return pl.pallas_call(
        functools.partial(_peer_dense_kernel, n_chunks=n_chunks),
        grid=(n // tb, n_chunks),
        in_specs=[
            pl.BlockSpec((D_MODEL, tb), lambda i, j: (0, i)),
            pl.BlockSpec((e, D_MODEL), lambda i, j: (j, 0)),
            pl.BlockSpec((1, D_MODEL, e), lambda i, j: (j, 0, 0)),
            tok, tok,
            rowspec, rowspec,
            pl.BlockSpec((tb, D_MODEL), lambda i, j: (i, 0), pipeline_mode=pl.Buffered(1)),
            pl.BlockSpec((1, D_MODEL), lambda i, j: (0, 0)),
            pl.BlockSpec((1, D_MODEL), lambda i, j: (0, 0)),
        ],
        out_specs=pl.BlockSpec((tb, D_MODEL), lambda i, j: (i, 0)),
        out_shape=jax.ShapeDtypeStruct((n, D_MODEL), F32),
        scratch_shapes=[
            pltpu.VMEM((D_MODEL, tb), F32),
            pltpu.VMEM((e, tb), F32),
            pltpu.VMEM((e, tb), BF16),
        ],
        compiler_params=_cparams(("parallel", "arbitrary")),
        name="peer_dense",
    )(x1t_bf, u_bf, vt_bf, rankt, e2t, by_head(e1t), by_head(cntt), x1, g, b)


def _rope_tables(pos):
    half = QK_DIM // 2
    inv = ROPE_THETA ** (-jnp.arange(half, dtype=F32) * 2.0 / QK_DIM)
    ang = pos.astype(F32)[:, None] * inv[None, :]
    cos, sin = jnp.cos(ang), jnp.sin(ang)
    reps = LANES // QK_DIM
    return (jnp.tile(jnp.concatenate([cos, cos], axis=1), (1, reps)),
            jnp.tile(jnp.concatenate([-sin, sin], axis=1), (1, reps)))


def _peer_mixer(x1, x1_bf, w, *, tm, tl, tb):
    st = _peer_scores(x1_bf, w["wq"], w["k1"], w["k2"], tm=tm)
    e1t, e2t, cntt, rankt = _peer_stats(st, tl=tl)
    return _peer_dense(x1_bf.T, w["u"], w["vt"], e1t, e2t, cntt, rankt, x1, w["ln2_g"], w["ln2_b"],
                       tb=tb)


def kernel(x_prompt, x_sample, cache_k, cache_v, state_conv, state_h, page_table, w_in, conv_w, conv_b, lru_wa, lru_ba, lru_wx, lru_bx, lru_lambda, lambda_q1, lambda_k1, lambda_q2, lambda_k2, subln_g, w_out, ln1_g, ln1_b, peer_wq, peer_k1, peer_k2, peer_u, peer_v, ln2_g, ln2_b):
    bsz, seq, _ = x_prompt.shape
    dbsz, dseq, _ = x_sample.shape
    n_pages = page_table.shape[1]
    page = cache_k.shape[2]
    past = n_pages * page
    n = bsz * seq
    l = 0

    w_in_bf = w_in[l].astype(BF16)
    w_rec = w_in_bf[:, :2 * D_REC]
    w_q = w_in_bf[:, 2 * D_REC:2 * D_REC + D_ATT]
    w_k = w_in_bf[:, 2 * D_REC + D_ATT:2 * D_REC + 2 * D_ATT]
    w_v = w_in_bf[:, 2 * D_REC + 2 * D_ATT:]
    w_out_bf = w_out[l].astype(BF16)
    wa_bf, wx_bf = lru_wa[l].astype(BF16), lru_wx[l].astype(BF16)
    row = lambda a: a[l].reshape(1, -1)
    lam_p = jnp.stack([lambda_q1[l], lambda_k1[l], lambda_q2[l], lambda_k2[l]])
    peer_w = dict(wq=peer_wq[l].astype(BF16), k1=peer_k1[l].astype(BF16), k2=peer_k2[l].astype(BF16),
                  u=peer_u[l].astype(BF16),
                  vt=peer_v[l].reshape(N_EXPERTS // PEER_CHUNK, PEER_CHUNK, D_MODEL)
                  .transpose(0, 2, 1).astype(BF16),
                  ln2_g=row(ln2_g), ln2_b=row(ln2_b))

    xp = x_prompt.reshape(n, D_MODEL)
    xp_bf = xp.astype(BF16)
    cos_p, sin_p = _rope_tables(jnp.arange(seq, dtype=jnp.int32))
    proj = functools.partial(_proj, xp_bf, cos_t=cos_p, sin_t=sin_p, tm=1024, tn=512)
    rec_p = proj(w_rec, rope=False, out_dtype=F32)
    q_p = proj(w_q, rope=True, out_dtype=BF16)
    k_p = proj(w_k, rope=True, out_dtype=F32)
    v_p = proj(w_v, rope=False, out_dtype=F32)
    rec_out_p, h_p = _rec_prompt(rec_p, conv_w[l], row(conv_b), wa_bf, wx_bf, row(lru_ba),
                                 row(lru_bx), row(lru_lambda), bsz=bsz, t=seq, cb=256)
    att_p = _attn_prompt(q_p, k_p, v_p, lam_p, row(subln_g), bsz=bsz, t=seq, tq=512)
    x1_p, x1_p_bf = _outproj(rec_out_p, att_p, w_out_bf[:D_REC], w_out_bf[D_REC:], xp,
                             row(ln1_g), row(ln1_b), tm=256)
    y_p = _peer_mixer(x1_p, x1_p_bf, peer_w, tm=256, tl=256, tb=512)

    xs = x_sample.reshape(dbsz * dseq, D_MODEL)
    ns = xs.shape[0]
    xs_bf = xs.astype(BF16)
    cos_s, sin_s = _rope_tables(jnp.full((ns,), past, dtype=jnp.int32))
    proj_s = functools.partial(_proj, xs_bf, cos_t=cos_s, sin_t=sin_s, tm=ns, tn=512)
    rec_s = proj_s(w_rec, rope=False, out_dtype=F32)
    q_s = proj_s(w_q, rope=True, out_dtype=F32)
    k_s = proj_s(w_k, rope=True, out_dtype=F32)
    v_s = proj_s(w_v, rope=False, out_dtype=F32)
    rec_out_s, conv_s_t, h_s = _rec_sample(
        rec_s, state_conv[l].transpose(1, 0, 2), state_h[l], conv_w[l], row(conv_b), wa_bf, wx_bf,
        row(lru_ba), row(lru_bx), row(lru_lambda))
    n_pool = cache_k.shape[1]
    att_s = _attn_sample(
        page_table, q_s.reshape(ns, N_SLOTS, QK_DIM), k_s.reshape(ns, N_SLOTS, QK_DIM),
        jnp.repeat(v_s.reshape(ns, N_ATT_HEADS, ATT_HEAD_DIM), 2, axis=1),
        cache_k[l].transpose(0, 2, 3, 1),
        cache_v[l].reshape(n_pool, page * N_ATT_HEADS, ATT_HEAD_DIM), lam_p, row(subln_g))
    pad = LANES - ns
    padr = lambda a: jnp.pad(a, ((0, pad), (0, 0)))
    x1_s, x1_s_bf = _outproj(padr(rec_out_s), padr(att_s.reshape(ns, D_ATT).astype(BF16)), w_out_bf[:D_REC],
                             w_out_bf[D_REC:], padr(xs), row(ln1_g), row(ln1_b), tm=LANES)
    y_s = _peer_mixer(x1_s, x1_s_bf, peer_w, tm=LANES, tl=LANES, tb=LANES)[:ns]

    k4 = (2 * N_ATT_HEADS, QK_DIM)
    v4 = (N_ATT_HEADS, ATT_HEAD_DIM)
    return (
        y_p.reshape(bsz, seq, D_MODEL),
        y_s.reshape(dbsz, dseq, D_MODEL),
        k_p.reshape(1, bsz, seq, *k4),
        v_p.reshape(1, bsz, seq, *v4),
        rec_p.reshape(bsz, seq, 2 * D_REC)[:, seq - (CONV_W - 1):, :D_REC][None],
        h_p.reshape(1, bsz, D_REC),
        k_s.reshape(1, dbsz, dseq, *k4),
        v_s.reshape(1, dbsz, dseq, *v4),
        conv_s_t.transpose(1, 0, 2)[None],
        h_s.reshape(1, dbsz, D_REC),
    )
```

```python
import functools
import itertools
import math

import jax
import jax.numpy as jnp
from jax import lax
from jax.experimental import pallas as pl
from jax.experimental.pallas import tpu as pltpu

F32 = jnp.float32
BF16 = jnp.bfloat16

D_MODEL = 2048
D_REC = 1024
D_ATT = 1024
N_REC_BLOCKS = 8
REC_BLOCK = 128
CONV_W = 4
LRU_C = 8.0
N_ATT_HEADS = 8
ATT_HEAD_DIM = 128
QK_DIM = 64
ROPE_THETA = 10000.0
N_KEYS = 128
N_EXPERTS = N_KEYS * N_KEYS
PEER_HEADS = 8
PEER_TOPK = 16
LN_EPS = 1e-5
NEG_INF = -1e30
DEPTH = 1
ALPHA = (2.0 * DEPTH) ** 0.25
LAM_INIT = 0.8 - 0.6 * math.exp(-0.3 * 0)
LANES = 128

VMEM_LIMIT = 56 * 1024 * 1024


def _cparams(sem, flags=None):
    return pltpu.CompilerParams(dimension_semantics=sem, vmem_limit_bytes=VMEM_LIMIT, flags=flags)


def _gelu(x):
    c = math.sqrt(2.0 / math.pi)
    return x * (0.5 * (1.0 + jnp.tanh(c * (x + 0.044715 * (x * x * x)))))


def _layer_norm(y, g, b):
    mu = jnp.mean(y, axis=-1, keepdims=True)
    yc = y - mu
    var = jnp.mean(yc * yc, axis=-1, keepdims=True)
    return yc * lax.rsqrt(var + LN_EPS) * g + b


def _proj_kernel(x_ref, w_ref, cos_ref, sin_ref, o_ref, *, rope):
    acc = jnp.dot(x_ref[...], w_ref[...], preferred_element_type=F32)
    if rope:
        c = cos_ref[...]
        s = sin_ref[...]
        lane = lax.broadcasted_iota(jnp.int32, (1, LANES), 1)
        first_half = (lane % QK_DIM) < (QK_DIM // 2)
        outs = []
        for j in range(acc.shape[1] // LANES):
            xc = acc[:, j * LANES:(j + 1) * LANES]
            fwd = pltpu.roll(xc, LANES - QK_DIM // 2, axis=1)
            bwd = pltpu.roll(xc, QK_DIM // 2, axis=1)
            partner = jnp.where(first_half, fwd, bwd)
            outs.append(xc * c + partner * s)
        acc = jnp.concatenate(outs, axis=1)
    o_ref[...] = acc.astype(o_ref.dtype)


def _proj(x_bf, w_bf, cos_t, sin_t, *, rope, out_dtype, tm, tn):
    n, k = x_bf.shape
    m = w_bf.shape[1]
    n_pos_blocks = cos_t.shape[0] // tm
    return pl.pallas_call(
        functools.partial(_proj_kernel, rope=rope),
        grid=(n // tm, m // tn),
        in_specs=[
            pl.BlockSpec((tm, k), lambda i, j: (i, 0)),
            pl.BlockSpec((k, tn), lambda i, j: (0, j)),
            pl.BlockSpec((tm, LANES), lambda i, j: (i % n_pos_blocks, 0)),
            pl.BlockSpec((tm, LANES), lambda i, j: (i % n_pos_blocks, 0)),
        ],
        out_specs=pl.BlockSpec((tm, tn), lambda i, j: (i, j)),
        out_shape=jax.ShapeDtypeStruct((n, m), out_dtype),
        compiler_params=_cparams(("parallel", "parallel")),
        name="in_proj_rope" if rope else "in_proj",
    )(x_bf, w_bf, cos_t, sin_t)


def _lru_gates(conv, wa_ref, wx_ref, ba, bx, lam):
    rs, is_ = [], []
    for blk in range(conv.shape[1] // REC_BLOCK):
        cb = conv[:, blk * REC_BLOCK:(blk + 1) * REC_BLOCK].astype(BF16)
        rs.append(jnp.dot(cb, wa_ref[blk], preferred_element_type=F32))
        is_.append(jnp.dot(cb, wx_ref[blk], preferred_element_type=F32))
    r = jax.nn.sigmoid(jnp.concatenate(rs, axis=1) + ba)
    i = jax.nn.sigmoid(jnp.concatenate(is_, axis=1) + bx)
    softplus_neg_lam = jnp.maximum(-lam, 0.0) + jnp.log1p(jnp.exp(-jnp.abs(lam)))
    log_a = -LRU_C * r * softplus_neg_lam
    a = jnp.exp(log_a)
    b = jnp.sqrt(1.0 - jnp.exp(2.0 * log_a)) * (i * conv)
    return a, b


def _shift_rows(x, s, fill):
    row = lax.broadcasted_iota(jnp.int32, x.shape, 0)
    return jnp.where(row >= s, pltpu.roll(x, s, axis=0), fill)


SCAN_GROUP = 8


def _rec_prompt_kernel(x_ref, g_ref, cw_ref, cb_ref, wa_ref, wx_ref, ba_ref, bx_ref, lam_ref,
                       o_ref, h_ref, a_sc, b_sc):
    x = x_ref[...]
    t = x.shape[0]
    cw = cw_ref[...]
    conv = (cb_ref[...] + cw[3:4] * x + cw[2:3] * _shift_rows(x, 1, 0.0)
            + cw[1:2] * _shift_rows(x, 2, 0.0) + cw[0:1] * _shift_rows(x, 3, 0.0))
    a, b = _lru_gates(conv, wa_ref, wx_ref, ba_ref[...], bx_ref[...], lam_ref[...])
    in_tile = lax.broadcasted_iota(jnp.int32, a.shape, 0) % SCAN_GROUP
    s = 1
    while s < SCAN_GROUP:
        b = a * jnp.where(in_tile >= s, pltpu.roll(b, s, axis=0), 0.0) + b
        a = a * jnp.where(in_tile >= s, pltpu.roll(a, s, axis=0), 1.0)
        s *= 2
    a_sc[...] = a
    b_sc[...] = b

    def tile(g, carry):
        rows = pl.ds(pl.multiple_of(g * SCAN_GROUP, SCAN_GROUP), SCAN_GROUP)
        h = a_sc[rows, :] * carry + b_sc[rows, :]
        b_sc[rows, :] = h
        return h[SCAN_GROUP - 1:SCAN_GROUP, :]

    h_last = lax.fori_loop(0, t // SCAN_GROUP, tile, jnp.zeros((1, x.shape[1]), F32), unroll=8)
    o_ref[...] = (b_sc[...] * _gelu(g_ref[...])).astype(o_ref.dtype)
    h_ref[0] = h_last


def _rec_prompt(rec, conv_w, conv_b, wa_bf, wx_bf, ba, bx, lam, *, bsz, t, cb):
    ncb = D_REC // cb
    return pl.pallas_call(
        _rec_prompt_kernel,
        grid=(bsz, ncb),
        in_specs=[
            pl.BlockSpec((t, cb), lambda b, c: (b, c)),
            pl.BlockSpec((t, cb), lambda b, c: (b, ncb + c)),
            pl.BlockSpec((CONV_W, cb), lambda b, c: (0, c)),
            pl.BlockSpec((1, cb), lambda b, c: (0, c)),
            pl.BlockSpec((cb // REC_BLOCK, REC_BLOCK, REC_BLOCK), lambda b, c: (c, 0, 0)),
            pl.BlockSpec((cb // REC_BLOCK, REC_BLOCK, REC_BLOCK), lambda b, c: (c, 0, 0)),
            pl.BlockSpec((1, cb), lambda b, c: (0, c)),
            pl.BlockSpec((1, cb), lambda b, c: (0, c)),
            pl.BlockSpec((1, cb), lambda b, c: (0, c)),
        ],
        out_specs=[
            pl.BlockSpec((t, cb), lambda b, c: (b, c)),
            pl.BlockSpec((1, 1, cb), lambda b, c: (b, 0, c)),
        ],
        out_shape=[
            jax.ShapeDtypeStruct((bsz * t, D_REC), BF16),
            jax.ShapeDtypeStruct((bsz, 1, D_REC), F32),
        ],
        scratch_shapes=[pltpu.VMEM((t, cb), F32), pltpu.VMEM((t, cb), F32)],
        compiler_params=_cparams(("parallel", "parallel")),
        name="rglru_prompt",
    )(rec, rec, conv_w, conv_b, wa_bf, wx_bf, ba, bx, lam)


def _rec_sample_kernel(rec_ref, sc_ref, h0_ref, cw_ref, cb_ref, wa_ref, wx_ref, ba_ref, bx_ref,
                       lam_ref, o_ref, nc_ref, h_ref):
    x = rec_ref[:, :D_REC]
    g = rec_ref[:, D_REC:]
    cw = cw_ref[...]
    conv = (cb_ref[...] + cw[0:1] * sc_ref[0] + cw[1:2] * sc_ref[1] + cw[2:3] * sc_ref[2]
            + cw[3:4] * x)
    a, b = _lru_gates(conv, wa_ref, wx_ref, ba_ref[...], bx_ref[...], lam_ref[...])
    h = a * h0_ref[...] + b
    o_ref[...] = (h * _gelu(g)).astype(o_ref.dtype)
    nc_ref[0] = sc_ref[1]
    nc_ref[1] = sc_ref[2]
    nc_ref[2] = x
    h_ref[...] = h


def _rec_sample(rec, sc_t, h0, conv_w, conv_b, wa_bf, wx_bf, ba, bx, lam):
    n = rec.shape[0]
    return pl.pallas_call(
        _rec_sample_kernel,
        out_shape=[
            jax.ShapeDtypeStruct((n, D_REC), BF16),
            jax.ShapeDtypeStruct((CONV_W - 1, n, D_REC), F32),
            jax.ShapeDtypeStruct((n, D_REC), F32),
        ],
        compiler_params=pltpu.CompilerParams(vmem_limit_bytes=VMEM_LIMIT),
        name="rglru_sample",
    )(rec, sc_t, h0, conv_w, conv_b, wa_bf, wx_bf, ba, bx, lam)


def _diff_lambda(lp):
    d1 = jnp.sum(lp[0:1] * lp[1:2], axis=1, keepdims=True)
    d2 = jnp.sum(lp[2:3] * lp[3:4], axis=1, keepdims=True)
    return jnp.exp(d1) - jnp.exp(d2) + LAM_INIT


def _nt_dot(a, b):
    return lax.dot_general(a, b, (((1,), (1,)), ((), ())), preferred_element_type=F32)


def _attn_prompt_kernel(q_ref, k_ref, v_ref, lp_ref, g_ref, o_ref, m_sc, l_sc, acc_sc, *, tq):
    qi = pl.program_id(2)
    lane = lax.broadcasted_iota(jnp.int32, (1, LANES), 1)
    qs = q_ref[...] * (QK_DIM ** -0.5)
    zero = jnp.zeros_like(qs)
    qmaps = (jnp.where(lane < QK_DIM, qs, zero), jnp.where(lane >= QK_DIM, qs, zero))
    m_sc[...] = jnp.full(m_sc.shape, NEG_INF, F32)
    l_sc[...] = jnp.zeros_like(l_sc)
    acc_sc[...] = jnp.zeros_like(acc_sc)
    reps = tq // LANES

    def block(kb, masked):
        start = pl.multiple_of(kb * tq, tq)
        k = k_ref[pl.ds(start, tq), :].astype(BF16)
        v = v_ref[pl.ds(start, tq), :].astype(BF16)
        for c in range(2):
            s = _nt_dot(qmaps[c], k)
            if masked:
                row = lax.broadcasted_iota(jnp.int32, s.shape, 0)
                col = lax.broadcasted_iota(jnp.int32, s.shape, 1)
                s = jnp.where(col <= row, s, NEG_INF)
            m_old = m_sc[c]
            m_new = jnp.maximum(m_old, jnp.max(s, axis=1, keepdims=True))
            alpha = jnp.exp(m_old - m_new)
            p = jnp.exp(s - jnp.tile(m_new, (1, reps)))
            l_sc[c] = alpha * l_sc[c] + jnp.sum(p, axis=1, keepdims=True)
            acc_sc[c] = alpha * acc_sc[c] + jnp.dot(p.astype(BF16), v, preferred_element_type=F32)
            m_sc[c] = m_new

    @pl.loop(0, qi)
    def _(kb):
        block(kb, False)

    block(qi, True)
    lam = _diff_lambda(lp_ref[...])
    att = acc_sc[0] / l_sc[0] - lam * (acc_sc[1] / l_sc[1])
    ms = jnp.mean(att * att, axis=1, keepdims=True)
    att = att * lax.rsqrt(ms + LN_EPS) * g_ref[...] * (1.0 - LAM_INIT)
    o_ref[...] = att.astype(o_ref.dtype)


def _attn_prompt(q_bf, k, v, lam_p, subln_g, *, bsz, t, tq):
    nq = t // tq
    return pl.pallas_call(
        functools.partial(_attn_prompt_kernel, tq=tq),
        grid=(bsz, N_ATT_HEADS, nq),
        in_specs=[
            pl.BlockSpec((tq, LANES), lambda b, h, i: (b * nq + i, h)),
            pl.BlockSpec((t, LANES), lambda b, h, i: (b, h)),
            pl.BlockSpec((t, LANES), lambda b, h, i: (b, h)),
            pl.BlockSpec((4, QK_DIM), lambda b, h, i: (0, 0)),
            pl.BlockSpec((1, ATT_HEAD_DIM), lambda b, h, i: (0, 0)),
        ],
        out_specs=pl.BlockSpec((tq, LANES), lambda b, h, i: (b * nq + i, h)),
        out_shape=jax.ShapeDtypeStruct((bsz * t, D_ATT), BF16),
        scratch_shapes=[
            pltpu.VMEM((2, tq, LANES), F32),
            pltpu.VMEM((2, tq, LANES), F32),
            pltpu.VMEM((2, tq, ATT_HEAD_DIM), F32),
        ],
        compiler_params=_cparams(("parallel", "parallel", "arbitrary")),
        name="diff_attn_prompt",
    )(q_bf, k, v, lam_p, subln_g)


N_SLOTS = 2 * N_ATT_HEADS
PAGES_PER_STEP = 16


def _attn_sample_kernel(pt_ref, q_ref, qb_ref, kn_ref, vn_ref, *refs, n_steps, pps, page):
    del pt_ref
    k_refs, v_refs = refs[:pps], refs[pps:2 * pps]
    lp_ref, g_ref, o_ref, m_sc, l_sc, acc_sc, o_sc = refs[2 * pps:]
    step = pl.program_id(1)
    q = q_ref[0] * (QK_DIM ** -0.5)

    @pl.when(step == 0)
    def _():
        s_self = jnp.sum(q * kn_ref[0], axis=1, keepdims=True)
        m_sc[...] = jnp.broadcast_to(s_self, m_sc.shape)
        l_sc[...] = jnp.ones_like(l_sc)
        acc_sc[...] = vn_ref[0]

    qb = qb_ref[0]
    slot_head = lax.broadcasted_iota(jnp.int32, (N_SLOTS, LANES), 0) // 2
    for i in range(pps):
        st = jnp.sum(k_refs[i][0] * qb, axis=1) * (QK_DIM ** -0.5)
        m_old = m_sc[...]
        m_new = jnp.maximum(m_old, jnp.max(st, axis=1, keepdims=True))
        alpha = jnp.exp(m_old - m_new)
        p = jnp.exp(st - m_new)
        l_sc[...] = alpha * l_sc[...] + jnp.sum(p, axis=1, keepdims=True)
        m_sc[...] = m_new
        p_bf = p.astype(BF16)
        pv = jnp.zeros((N_SLOTS, ATT_HEAD_DIM), F32)
        for h in range(N_ATT_HEADS):
            v_h = v_refs[i][0, pl.ds(h, page, stride=N_ATT_HEADS), :].astype(BF16)
            full = jnp.dot(p_bf, v_h, preferred_element_type=F32)
            pv = jnp.where(slot_head == h, full, pv)
        acc_sc[...] = alpha * acc_sc[...] + pv

    @pl.when(step == n_steps - 1)
    def _():
        lam = _diff_lambda(lp_ref[...])
        o_sc[...] = acc_sc[...] / l_sc[...]
        o1 = o_sc[pl.ds(0, N_ATT_HEADS, stride=2), :]
        o2 = o_sc[pl.ds(1, N_ATT_HEADS, stride=2), :]
        att = o1 - lam * o2
        ms = jnp.mean(att * att, axis=1, keepdims=True)
        o_ref[0] = att * lax.rsqrt(ms + LN_EPS) * g_ref[...] * (1.0 - LAM_INIT)


def _attn_sample(page_table, q, k_new, v_new2, cache_kt, cache_v2, lam_p, subln_g):
    nb, n_pages = page_table.shape
    pps = PAGES_PER_STEP
    page = cache_kt.shape[3]
    qb = jnp.broadcast_to(q[..., None], (*q.shape, page))
    n_steps = n_pages // pps
    kspec = lambda i: pl.BlockSpec((1, N_SLOTS, QK_DIM, page),
                                   lambda b, s, pt: (pt[b, s * pps + i], 0, 0, 0))
    vspec = lambda i: pl.BlockSpec((1, page * N_ATT_HEADS, ATT_HEAD_DIM),
                                   lambda b, s, pt: (pt[b, s * pps + i], 0, 0))
    grid_spec = pltpu.PrefetchScalarGridSpec(
        num_scalar_prefetch=1,
        grid=(nb, n_steps),
        in_specs=[
            pl.BlockSpec((1, N_SLOTS, QK_DIM), lambda b, s, pt: (b, 0, 0)),
            pl.BlockSpec((1, N_SLOTS, QK_DIM, page), lambda b, s, pt: (b, 0, 0, 0)),
            pl.BlockSpec((1, N_SLOTS, QK_DIM), lambda b, s, pt: (b, 0, 0)),
            pl.BlockSpec((1, N_SLOTS, ATT_HEAD_DIM), lambda b, s, pt: (b, 0, 0)),
            *[kspec(i) for i in range(pps)],
            *[vspec(i) for i in range(pps)],
            pl.BlockSpec((4, QK_DIM), lambda b, s, pt: (0, 0)),
            pl.BlockSpec((1, ATT_HEAD_DIM), lambda b, s, pt: (0, 0)),
        ],
        out_specs=pl.BlockSpec((1, N_ATT_HEADS, ATT_HEAD_DIM), lambda b, s, pt: (b, 0, 0)),
        scratch_shapes=[
            pltpu.VMEM((N_SLOTS, LANES), F32),
            pltpu.VMEM((N_SLOTS, LANES), F32),
            pltpu.VMEM((N_SLOTS, ATT_HEAD_DIM), F32),
            pltpu.VMEM((N_SLOTS, ATT_HEAD_DIM), F32),
        ],
    )
    return pl.pallas_call(
        functools.partial(_attn_sample_kernel, n_steps=n_steps, pps=pps, page=page),
        grid_spec=grid_spec,
        out_shape=jax.ShapeDtypeStruct((nb, N_ATT_HEADS, ATT_HEAD_DIM), F32),
        compiler_params=_cparams(("parallel", "arbitrary")),
        name="diff_attn_sample",
    )(page_table, q, qb, k_new, v_new2, *([cache_kt] * pps), *([cache_v2] * pps), lam_p, subln_g)


def _outproj_kernel(rec_ref, att_ref, wr_ref, wa_ref, x_ref, g_ref, b_ref, o_ref, ob_ref):
    mix = jnp.dot(rec_ref[...], wr_ref[...], preferred_element_type=F32)
    mix = mix + jnp.dot(att_ref[...], wa_ref[...], preferred_element_type=F32)
    y = _layer_norm(ALPHA * x_ref[...] + mix, g_ref[...], b_ref[...])
    o_ref[...] = y
    ob_ref[...] = y.astype(BF16)


def _outproj(rec_o, att_o, w_rec, w_att, x, g, b, *, tm):
    n = x.shape[0]
    return pl.pallas_call(
        _outproj_kernel,
        grid=(n // tm,),
        in_specs=[
            pl.BlockSpec((tm, D_REC), lambda i: (i, 0)),
            pl.BlockSpec((tm, D_ATT), lambda i: (i, 0)),
            pl.BlockSpec((D_REC, D_MODEL), lambda i: (0, 0)),
            pl.BlockSpec((D_ATT, D_MODEL), lambda i: (0, 0)),
            pl.BlockSpec((tm, D_MODEL), lambda i: (i, 0)),
            pl.BlockSpec((1, D_MODEL), lambda i: (0, 0)),
            pl.BlockSpec((1, D_MODEL), lambda i: (0, 0)),
        ],
        out_specs=[
            pl.BlockSpec((tm, D_MODEL), lambda i: (i, 0)),
            pl.BlockSpec((tm, D_MODEL), lambda i: (i, 0)),
        ],
        out_shape=[
            jax.ShapeDtypeStruct((n, D_MODEL), F32),
            jax.ShapeDtypeStruct((n, D_MODEL), BF16),
        ],
        compiler_params=_cparams(("parallel",)),
        name="out_proj_ln",
    )(rec_o, att_o, w_rec, w_att, x, g, b)


def _peer_scores_kernel(x_ref, wq_ref, k1_ref, k2_ref, o_ref):
    q = jnp.dot(x_ref[...], wq_ref[...], preferred_element_type=F32).astype(BF16)
    for h in range(PEER_HEADS):
        for c, k_ref in enumerate((k1_ref, k2_ref)):
            qb = q[:, (2 * h + c) * N_KEYS:(2 * h + c + 1) * N_KEYS]
            base = (c * PEER_HEADS + h) * N_KEYS
            o_ref[base:base + N_KEYS, :] = _nt_dot(k_ref[h], qb)


def _peer_scores(x1_bf, wq_bf, k1_bf, k2_bf, *, tm):
    n = x1_bf.shape[0]
    rows = 2 * PEER_HEADS * N_KEYS
    return pl.pallas_call(
        _peer_scores_kernel,
        grid=(n // tm,),
        in_specs=[
            pl.BlockSpec((tm, D_MODEL), lambda i: (i, 0)),
            pl.BlockSpec((D_MODEL, rows), lambda i: (0, 0)),
            pl.BlockSpec((PEER_HEADS, N_KEYS, N_KEYS), lambda i: (0, 0, 0)),
            pl.BlockSpec((PEER_HEADS, N_KEYS, N_KEYS), lambda i: (0, 0, 0)),
        ],
        out_specs=pl.BlockSpec((rows, tm), lambda i: (0, i)),
        out_shape=jax.ShapeDtypeStruct((rows, n), F32),
        compiler_params=_cparams(("parallel",)),
        name="peer_scores",
    )(x1_bf, wq_bf, k1_bf, k2_bf)


def _top_values(x, k):
    vals = []
    for _ in range(k):
        cur = jnp.max(x, axis=0, keepdims=True)
        vals.append(cur)
        x = jnp.where(x == cur, -jnp.inf, x)
    return jnp.concatenate(vals, axis=0)


def _peer_stats_kernel(s_ref, e1_ref, e2_ref, cnt_ref, rank_ref):
    half = PEER_HEADS * N_KEYS
    k = PEER_TOPK
    for h in range(PEER_HEADS):
        r1 = slice(h * N_KEYS, (h + 1) * N_KEYS)
        s1 = s_ref[r1, :]
        s2 = s_ref[half + h * N_KEYS:half + (h + 1) * N_KEYS, :]
        v1 = _top_values(s1, k)
        v2 = _top_values(s2, k)
        sub = lax.broadcasted_iota(jnp.int32, (8, 1), 0)
        cands = [v1[0:8] + v2[0:1], v1[8:16] + v2[0:1], v2[8:16] + v1[0:1]]
        for b in range(1, 8):
            n_a = k // (b + 1)
            cands.append(jnp.where(sub < n_a, v1[0:8] + v2[b:b + 1], -jnp.inf))
        cand = jnp.concatenate(cands, axis=0)
        work = cand
        theta = None
        for _ in range(k):
            theta = jnp.max(work, axis=0, keepdims=True)
            work = jnp.where(work == theta, -jnp.inf, work)
        m = v1[0:1] + v2[0:1]
        z = jnp.sum(jnp.where(cand >= theta, jnp.exp(cand - m), 0.0), axis=0, keepdims=True)
        e1_ref[r1, :] = jnp.exp(s1 - (v1[0:1] + jnp.log(z)))
        e2_ref[r1, :] = jnp.exp(s2 - v2[0:1]).astype(BF16)
        cnt = jnp.zeros(s1.shape, F32)
        rank = jnp.zeros(s2.shape, F32)
        for b in range(k):
            cnt = cnt + jnp.where(s1 + v2[b:b + 1] >= theta, 1.0, 0.0)
            rank = rank + jnp.where(v2[b:b + 1] > s2, 1.0, 0.0)
        cnt_ref[r1, :] = cnt
        rank_ref[r1, :] = rank.astype(BF16)


def _peer_stats(st, *, tl):
    rows, n = st.shape
    half = rows // 2
    spec = pl.BlockSpec((half, tl), lambda i: (0, i))
    f32, bf16 = jax.ShapeDtypeStruct((half, n), F32), jax.ShapeDtypeStruct((half, n), BF16)
    return pl.pallas_call(
        _peer_stats_kernel,
        grid=(n // tl,),
        in_specs=[pl.BlockSpec((rows, tl), lambda i: (0, i))],
        out_specs=[spec] * 4,
        out_shape=[f32, bf16, f32, bf16],
        compiler_params=_cparams(("parallel",)),
        name="peer_stats",
    )(st)


PEER_CHUNK = 8 * N_KEYS
PEER_SUB = 4 * N_KEYS
GATE_ROWS = 64


def _peer_dense_kernel(xt_ref, u_ref, vt_ref, rank_ref, e2_ref, e1_ref, cnt_ref, x_ref, g_ref,
                       b_ref, o_ref, acc_ref, sc_ref, p_ref, *, n_chunks):
    j = pl.program_id(1)
    tb = p_ref.shape[1]

    @pl.when(j == 0)
    def _():
        acc_ref[...] = jnp.zeros_like(acc_ref)

    xt = xt_ref[...]
    n_sub = PEER_CHUNK // PEER_SUB
    subs = [slice(c * PEER_SUB, (c + 1) * PEER_SUB) for c in range(n_sub)]
    sc_ref[subs[0], :] = jnp.dot(u_ref[subs[0], :], xt, preferred_element_type=F32)
    for c in range(n_sub):
        if c + 1 < n_sub:
            sc_ref[subs[c + 1], :] = jnp.dot(u_ref[subs[c + 1], :], xt, preferred_element_type=F32)
        for r in range(c * (PEER_SUB // N_KEYS), (c + 1) * (PEER_SUB // N_KEYS)):
            for i2h, lg in itertools.product(range(N_KEYS // GATE_ROWS), range(tb // LANES)):
                cols = slice(lg * LANES, (lg + 1) * LANES)
                rows = slice(r * N_KEYS + i2h * GATE_ROWS, r * N_KEYS + (i2h + 1) * GATE_ROWS)
                w = None
                for h in range(PEER_HEADS):
                    hr = slice(h * N_KEYS + i2h * GATE_ROWS, h * N_KEYS + (i2h + 1) * GATE_ROWS)
                    cnt = cnt_ref[h, r:r + 1, cols].astype(BF16)
                    e1 = e1_ref[h, r:r + 1, cols].astype(BF16)
                    gate = jnp.where(rank_ref[hr, cols] < cnt, e2_ref[hr, cols] * e1,
                                     jnp.zeros((), BF16))
                    w = gate if w is None else w + gate
                p_ref[rows, cols] = w * _gelu(sc_ref[rows, cols]).astype(BF16)
        acc_ref[...] += jnp.dot(vt_ref[0, :, subs[c]], p_ref[subs[c], :],
                                preferred_element_type=F32)

    @pl.when(j == n_chunks - 1)
    def _():
        o_ref[...] = _layer_norm(ALPHA * x_ref[...] + acc_ref[...].T, g_ref[...], b_ref[...])


def _peer_dense(x1t_bf, u_bf, vt_bf, e1t, e2t, cntt, rankt, x1, g, b, *, tb):
    n = x1.shape[0]
    e = PEER_CHUNK
    n_chunks = N_EXPERTS // e
    rpc = e // N_KEYS
    half = PEER_HEADS * N_KEYS
    tok = pl.BlockSpec((half, tb), lambda i, j: (0, i))
    by_head = lambda a: a.reshape(PEER_HEADS, N_KEYS, n)
    rowspec = pl.BlockSpec((PEER_HEADS, rpc, tb), lambda i, j: (0, j, i))
    return pl.pallas_call(
        functools.partial(_peer_dense_kernel, n_chunks=n_chunks),
        grid=(n // tb, n_chunks),
        in_specs=[
            pl.BlockSpec((D_MODEL, tb), lambda i, j: (0, i)),
            pl.BlockSpec((e, D_MODEL), lambda i, j: (j, 0)),
            pl.BlockSpec((1, D_MODEL, e), lambda i, j: (j, 0, 0)),
            tok, tok,
            rowspec, rowspec,
            pl.BlockSpec((tb, D_MODEL), lambda i, j: (i, 0), pipeline_mode=pl.Buffered(1)),
            pl.BlockSpec((1, D_MODEL), lambda i, j: (0, 0)),
            pl.BlockSpec((1, D_MODEL), lambda i, j: (0, 0)),
        ],
        out_specs=pl.BlockSpec((tb, D_MODEL), lambda i, j: (i, 0)),
        out_shape=jax.ShapeDtypeStruct((n, D_MODEL), F32),
        scratch_shapes=[
            pltpu.VMEM((D_MODEL, tb), F32),
            pltpu.VMEM((e, tb), F32),
            pltpu.VMEM((e, tb), BF16),
        ],
        compiler_params=_cparams(("parallel", "arbitrary")),
        name="peer_dense",
    )(x1t_bf, u_bf, vt_bf, rankt, e2t, by_head(e1t), by_head(cntt), x1, g, b)


def _rope_tables(pos):
    half = QK_DIM // 2
    inv = ROPE_THETA ** (-jnp.arange(half, dtype=F32) * 2.0 / QK_DIM)
    ang = pos.astype(F32)[:, None] * inv[None, :]
    cos, sin = jnp.cos(ang), jnp.sin(ang)
    reps = LANES // QK_DIM
    return (jnp.tile(jnp.concatenate([cos, cos], axis=1), (1, reps)),
            jnp.tile(jnp.concatenate([-sin, sin], axis=1), (1, reps)))


def _peer_mixer(x1, x1_bf, w, *, tm, tl, tb):
    st = _peer_scores(x1_bf, w["wq"], w["k1"], w["k2"], tm=tm)
    e1t, e2t, cntt, rankt = _peer_stats(st, tl=tl)
    return _peer_dense(x1_bf.T, w["u"], w["vt"], e1t, e2t, cntt, rankt, x1, w["ln2_g"], w["ln2_b"],
                       tb=tb)


def kernel(x_prompt, x_sample, cache_k, cache_v, state_conv, state_h, page_table, w_in, conv_w, conv_b, lru_wa, lru_ba, lru_wx, lru_bx, lru_lambda, lambda_q1, lambda_k1, lambda_q2, lambda_k2, subln_g, w_out, ln1_g, ln1_b, peer_wq, peer_k1, peer_k2, peer_u, peer_v, ln2_g, ln2_b):
    bsz, seq, _ = x_prompt.shape
    dbsz, dseq, _ = x_sample.shape
    n_pages = page_table.shape[1]
    page = cache_k.shape[2]
    past = n_pages * page
    n = bsz * seq
    l = 0

    w_in_bf = w_in[l].astype(BF16)
    w_rec = w_in_bf[:, :2 * D_REC]
    w_q = w_in_bf[:, 2 * D_REC:2 * D_REC + D_ATT]
    w_k = w_in_bf[:, 2 * D_REC + D_ATT:2 * D_REC + 2 * D_ATT]
    w_v = w_in_bf[:, 2 * D_REC + 2 * D_ATT:]
    w_out_bf = w_out[l].astype(BF16)
    wa_bf, wx_bf = lru_wa[l].astype(BF16), lru_wx[l].astype(BF16)
    row = lambda a: a[l].reshape(1, -1)
    lam_p = jnp.stack([lambda_q1[l], lambda_k1[l], lambda_q2[l], lambda_k2[l]])
    peer_w = dict(wq=peer_wq[l].astype(BF16), k1=peer_k1[l].astype(BF16), k2=peer_k2[l].astype(BF16),
                  u=peer_u[l].astype(BF16),
                  vt=peer_v[l].reshape(N_EXPERTS // PEER_CHUNK, PEER_CHUNK, D_MODEL)
                  .transpose(0, 2, 1).astype(BF16),
                  ln2_g=row(ln2_g), ln2_b=row(ln2_b))

    xp = x_prompt.reshape(n, D_MODEL)
    xp_bf = xp.astype(BF16)
    cos_p, sin_p = _rope_tables(jnp.arange(seq, dtype=jnp.int32))
    proj = functools.partial(_proj, xp_bf, cos_t=cos_p, sin_t=sin_p, tm=1024, tn=512)
    rec_p = proj(w_rec, rope=False, out_dtype=F32)
    q_p = proj(w_q, rope=True, out_dtype=BF16)
    k_p = proj(w_k, rope=True, out_dtype=F32)
    v_p = proj(w_v, rope=False, out_dtype=F32)
    rec_out_p, h_p = _rec_prompt(rec_p, conv_w[l], row(conv_b), wa_bf, wx_bf, row(lru_ba),
                                 row(lru_bx), row(lru_lambda), bsz=bsz, t=seq, cb=256)
    att_p = _attn_prompt(q_p, k_p, v_p, lam_p, row(subln_g), bsz=bsz, t=seq, tq=512)
    x1_p, x1_p_bf = _outproj(rec_out_p, att_p, w_out_bf[:D_REC], w_out_bf[D_REC:], xp,
                             row(ln1_g), row(ln1_b), tm=256)
    y_p = _peer_mixer(x1_p, x1_p_bf, peer_w, tm=256, tl=256, tb=512)

    xs = x_sample.reshape(dbsz * dseq, D_MODEL)
    ns = xs.shape[0]
    xs_bf = xs.astype(BF16)
    cos_s, sin_s = _rope_tables(jnp.full((ns,), past, dtype=jnp.int32))
    proj_s = functools.partial(_proj, xs_bf, cos_t=cos_s, sin_t=sin_s, tm=ns, tn=512)
    rec_s = proj_s(w_rec, rope=False, out_dtype=F32)
    q_s = proj_s(w_q, rope=True, out_dtype=F32)
    k_s = proj_s(w_k, rope=True, out_dtype=F32)
    v_s = proj_s(w_v, rope=False, out_dtype=F32)
    rec_out_s, conv_s_t, h_s = _rec_sample(
        rec_s, state_conv[l].transpose(1, 0, 2), state_h[l], conv_w[l], row(conv_b), wa_bf, wx_bf,
        row(lru_ba), row(lru_bx), row(lru_lambda))
    n_pool = cache_k.shape[1]
    att_s = _attn_sample(
        page_table, q_s.reshape(ns, N_SLOTS, QK_DIM), k_s.reshape(ns, N_SLOTS, QK_DIM),
        jnp.repeat(v_s.reshape(ns, N_ATT_HEADS, ATT_HEAD_DIM), 2, axis=1),
        cache_k[l].transpose(0, 2, 3, 1),
        cache_v[l].reshape(n_pool, page * N_ATT_HEADS, ATT_HEAD_DIM), lam_p, row(subln_g))
    pad = LANES - ns
    padr = lambda a: jnp.pad(a, ((0, pad), (0, 0)))
    x1_s, x1_s_bf = _outproj(padr(rec_out_s), padr(att_s.reshape(ns, D_ATT).astype(BF16)), w_out_bf[:D_REC],
                             w_out_bf[D_REC:], padr(xs), row(ln1_g), row(ln1_b), tm=LANES)
    y_s = _peer_mixer(x1_s, x1_s_bf, peer_w, tm=LANES, tl=LANES, tb=LANES)[:ns]

    k4 = (2 * N_ATT_HEADS, QK_DIM)
    v4 = (N_ATT_HEADS, ATT_HEAD_DIM)
    return (
        y_p.reshape(bsz, seq, D_MODEL),
        y_s.reshape(dbsz, dseq, D_MODEL),
        k_p.reshape(1, bsz, seq, *k4),
        v_p.reshape(1, bsz, seq, *v4),
        rec_p.reshape(bsz, seq, 2 * D_REC)[:, seq - (CONV_W - 1):, :D_REC][None],
        h_p.reshape(1, bsz, D_REC),
        k_s.reshape(1, dbsz, dseq, *k4),
        v_s.reshape(1, dbsz, dseq, *v4),
        conv_s_t.transpose(1, 0, 2)[None],
        h_s.reshape(1, dbsz, D_REC),
    )
```

```python
import functools
import itertools
import math

import jax
import jax.numpy as jnp
from jax import lax
from jax.experimental import pallas as pl
from jax.experimental.pallas import tpu as pltpu

F32 = jnp.float32
BF16 = jnp.bfloat16

D_MODEL = 2048
D_REC = 1024
D_ATT = 1024
N_REC_BLOCKS = 8
REC_BLOCK = 128
CONV_W = 4
LRU_C = 8.0
N_ATT_HEADS = 8
ATT_HEAD_DIM = 128
QK_DIM = 64
ROPE_THETA = 10000.0
N_KEYS = 128
N_EXPERTS = N_KEYS * N_KEYS
PEER_HEADS = 8
PEER_TOPK = 16
LN_EPS = 1e-5
NEG_INF = -1e30
DEPTH = 1
ALPHA = (2.0 * DEPTH) ** 0.25
LAM_INIT = 0.8 - 0.6 * math.exp(-0.3 * 0)
LANES = 128

VMEM_LIMIT = 56 * 1024 * 1024


def _cparams(sem, flags=None):
    return pltpu.CompilerParams(dimension_semantics=sem, vmem_limit_bytes=VMEM_LIMIT, flags=flags)


def _gelu(x):
    c = math.sqrt(2.0 / math.pi)
    return x * (0.5 * (1.0 + jnp.tanh(c * (x + 0.044715 * (x * x * x)))))


def _layer_norm(y, g, b):
    mu = jnp.mean(y, axis=-1, keepdims=True)
    yc = y - mu
    var = jnp.mean(yc * yc, axis=-1, keepdims=True)
    return yc * lax.rsqrt(var + LN_EPS) * g + b


def _proj_kernel(x_ref, w_ref, cos_ref, sin_ref, o_ref, *, rope):
    acc = jnp.dot(x_ref[...], w_ref[...], preferred_element_type=F32)
    if rope:
        c = cos_ref[...]
        s = sin_ref[...]
        lane = lax.broadcasted_iota(jnp.int32, (1, LANES), 1)
        first_half = (lane % QK_DIM) < (QK_DIM // 2)
        outs = []
        for j in range(acc.shape[1] // LANES):
            xc = acc[:, j * LANES:(j + 1) * LANES]
            fwd = pltpu.roll(xc, LANES - QK_DIM // 2, axis=1)
            bwd = pltpu.roll(xc, QK_DIM // 2, axis=1)
            partner = jnp.where(first_half, fwd, bwd)
            outs.append(xc * c + partner * s)
        acc = jnp.concatenate(outs, axis=1)
    o_ref[...] = acc.astype(o_ref.dtype)


def _proj(x_bf, w_bf, cos_t, sin_t, *, rope, out_dtype, tm, tn):
    n, k = x_bf.shape
    m = w_bf.shape[1]
    n_pos_blocks = cos_t.shape[0] // tm
    return pl.pallas_call(
        functools.partial(_proj_kernel, rope=rope),
        grid=(n // tm, m // tn),
        in_specs=[
            pl.BlockSpec((tm, k), lambda i, j: (i, 0)),
            pl.BlockSpec((k, tn), lambda i, j: (0, j)),
            pl.BlockSpec((tm, LANES), lambda i, j: (i % n_pos_blocks, 0)),
            pl.BlockSpec((tm, LANES), lambda i, j: (i % n_pos_blocks, 0)),
        ],
        out_specs=pl.BlockSpec((tm, tn), lambda i, j: (i, j)),
        out_shape=jax.ShapeDtypeStruct((n, m), out_dtype),
        compiler_params=_cparams(("parallel", "parallel")),
        name="in_proj_rope" if rope else "in_proj",
    )(x_bf, w_bf, cos_t, sin_t)


def _lru_gates(conv, wa_ref, wx_ref, ba, bx, lam):
    rs, is_ = [], []
    for blk in range(conv.shape[1] // REC_BLOCK):
        cb = conv[:, blk * REC_BLOCK:(blk + 1) * REC_BLOCK].astype(BF16)
        rs.append(jnp.dot(cb, wa_ref[blk], preferred_element_type=F32))
        is_.append(jnp.dot(cb, wx_ref[blk], preferred_element_type=F32))
    r = jax.nn.sigmoid(jnp.concatenate(rs, axis=1) + ba)
    i = jax.nn.sigmoid(jnp.concatenate(is_, axis=1) + bx)
    softplus_neg_lam = jnp.maximum(-lam, 0.0) + jnp.log1p(jnp.exp(-jnp.abs(lam)))
    log_a = -LRU_C * r * softplus_neg_lam
    a = jnp.exp(log_a)
    b = jnp.sqrt(1.0 - jnp.exp(2.0 * log_a)) * (i * conv)
    return a, b


def _shift_rows(x, s, fill):
    row = lax.broadcasted_iota(jnp.int32, x.shape, 0)
    return jnp.where(row >= s, pltpu.roll(x, s, axis=0), fill)


SCAN_GROUP = 8


def _rec_prompt_kernel(x_ref, g_ref, cw_ref, cb_ref, wa_ref, wx_ref, ba_ref, bx_ref, lam_ref,
                       o_ref, h_ref, a_sc, b_sc):
    x = x_ref[...]
    t = x.shape[0]
    cw = cw_ref[...]
    conv = (cb_ref[...] + cw[3:4] * x + cw[2:3] * _shift_rows(x, 1, 0.0)
            + cw[1:2] * _shift_rows(x, 2, 0.0) + cw[0:1] * _shift_rows(x, 3, 0.0))
    a, b = _lru_gates(conv, wa_ref, wx_ref, ba_ref[...], bx_ref[...], lam_ref[...])
    in_tile = lax.broadcasted_iota(jnp.int32, a.shape, 0) % SCAN_GROUP
    s = 1
    while s < SCAN_GROUP:
        b = a * jnp.where(in_tile >= s, pltpu.roll(b, s, axis=0), 0.0) + b
        a = a * jnp.where(in_tile >= s, pltpu.roll(a, s, axis=0), 1.0)
        s *= 2
    a_sc[...] = a
    b_sc[...] = b

    def tile(g, carry):
        rows = pl.ds(pl.multiple_of(g * SCAN_GROUP, SCAN_GROUP), SCAN_GROUP)
        h = a_sc[rows, :] * carry + b_sc[rows, :]
        b_sc[rows, :] = h
        return h[SCAN_GROUP - 1:SCAN_GROUP, :]

    h_last = lax.fori_loop(0, t // SCAN_GROUP, tile, jnp.zeros((1, x.shape[1]), F32), unroll=8)
    o_ref[...] = (b_sc[...] * _gelu(g_ref[...])).astype(o_ref.dtype)
    h_ref[0] = h_last


def _rec_prompt(rec, conv_w, conv_b, wa_bf, wx_bf, ba, bx, lam, *, bsz, t, cb):
    ncb = D_REC // cb
    return pl.pallas_call(
        _rec_prompt_kernel,
        grid=(bsz, ncb),
        in_specs=[
            pl.BlockSpec((t, cb), lambda b, c: (b, c)),
            pl.BlockSpec((t, cb), lambda b, c: (b, ncb + c)),
            pl.BlockSpec((CONV_W, cb), lambda b, c: (0, c)),
            pl.BlockSpec((1, cb), lambda b, c: (0, c)),
            pl.BlockSpec((cb // REC_BLOCK, REC_BLOCK, REC_BLOCK), lambda b, c: (c, 0, 0)),
            pl.BlockSpec((cb // REC_BLOCK, REC_BLOCK, REC_BLOCK), lambda b, c: (c, 0, 0)),
            pl.BlockSpec((1, cb), lambda b, c: (0, c)),
            pl.BlockSpec((1, cb), lambda b, c: (0, c)),
            pl.BlockSpec((1, cb), lambda b, c: (0, c)),
        ],
        out_specs=[
            pl.BlockSpec((t, cb), lambda b, c: (b, c)),
            pl.BlockSpec((1, 1, cb), lambda b, c: (b, 0, c)),
        ],
        out_shape=[
            jax.ShapeDtypeStruct((bsz * t, D_REC), BF16),
            jax.ShapeDtypeStruct((bsz, 1, D_REC), F32),
        ],
        scratch_shapes=[pltpu.VMEM((t, cb), F32), pltpu.VMEM((t, cb), F32)],
        compiler_params=_cparams(("parallel", "parallel")),
        name="rglru_prompt",
    )(rec, rec, conv_w, conv_b, wa_bf, wx_bf, ba, bx, lam)


def _rec_sample_kernel(rec_ref, sc_ref, h0_ref, cw_ref, cb_ref, wa_ref, wx_ref, ba_ref, bx_ref,
                       lam_ref, o_ref, nc_ref, h_ref):
    x = rec_ref[:, :D_REC]
    g = rec_ref[:, D_REC:]
    cw = cw_ref[...]
    conv = (cb_ref[...] + cw[0:1] * sc_ref[0] + cw[1:2] * sc_ref[1] + cw[2:3] * sc_ref[2]
            + cw[3:4] * x)
    a, b = _lru_gates(conv, wa_ref, wx_ref, ba_ref[...], bx_ref[...], lam_ref[...])
    h = a * h0_ref[...] + b
    o_ref[...] = (h * _gelu(g)).astype(o_ref.dtype)
    nc_ref[0] = sc_ref[1]
    nc_ref[1] = sc_ref[2]
    nc_ref[2] = x
    h_ref[...] = h


def _rec_sample(rec, sc_t, h0, conv_w, conv_b, wa_bf, wx_bf, ba, bx, lam):
    n = rec.shape[0]
    return pl.pallas_call(
        _rec_sample_kernel,
        out_shape=[
            jax.ShapeDtypeStruct((n, D_REC), BF16),
            jax.ShapeDtypeStruct((CONV_W - 1, n, D_REC), F32),
            jax.ShapeDtypeStruct((n, D_REC), F32),
        ],
        compiler_params=pltpu.CompilerParams(vmem_limit_bytes=VMEM_LIMIT),
        name="rglru_sample",
    )(rec, sc_t, h0, conv_w, conv_b, wa_bf, wx_bf, ba, bx, lam)


def _diff_lambda(lp):
    d1 = jnp.sum(lp[0:1] * lp[1:2], axis=1, keepdims=True)
    d2 = jnp.sum(lp[2:3] * lp[3:4], axis=1, keepdims=True)
    return jnp.exp(d1) - jnp.exp(d2) + LAM_INIT


def _nt_dot(a, b):
    return lax.dot_general(a, b, (((1,), (1,)), ((), ())), preferred_element_type=F32)


def _attn_prompt_kernel(q_ref, k_ref, v_ref, lp_ref, g_ref, o_ref, m_sc, l_sc, acc_sc, *, tq):
    qi = pl.program_id(2)
    lane = lax.broadcasted_iota(jnp.int32, (1, LANES), 1)
    qs = q_ref[...] * (QK_DIM ** -0.5)
    zero = jnp.zeros_like(qs)
    qmaps = (jnp.where(lane < QK_DIM, qs, zero), jnp.where(lane >= QK_DIM, qs, zero))
    m_sc[...] = jnp.full(m_sc.shape, NEG_INF, F32)
    l_sc[...] = jnp.zeros_like(l_sc)
    acc_sc[...] = jnp.zeros_like(acc_sc)
    reps = tq // LANES

    def block(kb, masked):
        start = pl.multiple_of(kb * tq, tq)
        k = k_ref[pl.ds(start, tq), :].astype(BF16)
        v = v_ref[pl.ds(start, tq), :].astype(BF16)
        for c in range(2):
            s = _nt_dot(qmaps[c], k)
            if masked:
                row = lax.broadcasted_iota(jnp.int32, s.shape, 0)
                col = lax.broadcasted_iota(jnp.int32, s.shape, 1)
                s = jnp.where(col <= row, s, NEG_INF)
            m_old = m_sc[c]
            m_new = jnp.maximum(m_old, jnp.max(s, axis=1, keepdims=True))
            alpha = jnp.exp(m_old - m_new)
            p = jnp.exp(s - jnp.tile(m_new, (1, reps)))
            l_sc[c] = alpha * l_sc[c] + jnp.sum(p, axis=1, keepdims=True)
            acc_sc[c] = alpha * acc_sc[c] + jnp.dot(p.astype(BF16), v, preferred_element_type=F32)
            m_sc[c] = m_new

    @pl.loop(0, qi)
    def _(kb):
        block(kb, False)

    block(qi, True)
    lam = _diff_lambda(lp_ref[...])
    att = acc_sc[0] / l_sc[0] - lam * (acc_sc[1] / l_sc[1])
    ms = jnp.mean(att * att, axis=1, keepdims=True)
    att = att * lax.rsqrt(ms + LN_EPS) * g_ref[...] * (1.0 - LAM_INIT)
    o_ref[...] = att.astype(o_ref.dtype)


def _attn_prompt(q_bf, k, v, lam_p, subln_g, *, bsz, t, tq):
    nq = t // tq
    return pl.pallas_call(
        functools.partial(_attn_prompt_kernel, tq=tq),
        grid=(bsz, N_ATT_HEADS, nq),
        in_specs=[
            pl.BlockSpec((tq, LANES), lambda b, h, i: (b * nq + i, h)),
            pl.BlockSpec((t, LANES), lambda b, h, i: (b, h)),
            pl.BlockSpec((t, LANES), lambda b, h, i: (b, h)),
            pl.BlockSpec((4, QK_DIM), lambda b, h, i: (0, 0)),
            pl.BlockSpec((1, ATT_HEAD_DIM), lambda b, h, i: (0, 0)),
        ],
        out_specs=pl.BlockSpec((tq, LANES), lambda b, h, i: (b * nq + i, h)),
        out_shape=jax.ShapeDtypeStruct((bsz * t, D_ATT), BF16),
        scratch_shapes=[
            pltpu.VMEM((2, tq, LANES), F32),
            pltpu.VMEM((2, tq, LANES), F32),
            pltpu.VMEM((2, tq, ATT_HEAD_DIM), F32),
        ],
        compiler_params=_cparams(("parallel", "parallel", "arbitrary")),
        name="diff_attn_prompt",
    )(q_bf, k, v, lam_p, subln_g)


N_SLOTS = 2 * N_ATT_HEADS
PAGES_PER_STEP = 16


def _attn_sample_kernel(pt_ref, q_ref, qb_ref, kn_ref, vn_ref, *refs, n_steps, pps, page):
    del pt_ref
    k_refs, v_refs = refs[:pps], refs[pps:2 * pps]
    lp_ref, g_ref, o_ref, m_sc, l_sc, acc_sc, o_sc = refs[2 * pps:]
    step = pl.program_id(1)
    q = q_ref[0] * (QK_DIM ** -0.5)

    @pl.when(step == 0)
    def _():
        s_self = jnp.sum(q * kn_ref[0], axis=1, keepdims=True)
        m_sc[...] = jnp.broadcast_to(s_self, m_sc.shape)
        l_sc[...] = jnp.ones_like(l_sc)
        acc_sc[...] = vn_ref[0]

    qb = qb_ref[0]
    slot_head = lax.broadcasted_iota(jnp.int32, (N_SLOTS, LANES), 0) // 2
    for i in range(pps):
        st = jnp.sum(k_refs[i][0] * qb, axis=1) * (QK_DIM ** -0.5)
        m_old = m_sc[...]
        m_new = jnp.maximum(m_old, jnp.max(st, axis=1, keepdims=True))
        alpha = jnp.exp(m_old - m_new)
        p = jnp.exp(st - m_new)
        l_sc[...] = alpha * l_sc[...] + jnp.sum(p, axis=1, keepdims=True)
        m_sc[...] = m_new
        p_bf = p.astype(BF16)
        pv = jnp.zeros((N_SLOTS, ATT_HEAD_DIM), F32)
        for h in range(N_ATT_HEADS):
            v_h = v_refs[i][0, pl.ds(h, page, stride=N_ATT_HEADS), :].astype(BF16)
            full = jnp.dot(p_bf, v_h, preferred_element_type=F32)
            pv = jnp.where(slot_head == h, full, pv)
        acc_sc[...] = alpha * acc_sc[...] + pv

    @pl.when(step == n_steps - 1)
    def _():
        lam = _diff_lambda(lp_ref[...])
        o_sc[...] = acc_sc[...] / l_sc[...]
        o1 = o_sc[pl.ds(0, N_ATT_HEADS, stride=2), :]
        o2 = o_sc[pl.ds(1, N_ATT_HEADS, stride=2), :]
        att = o1 - lam * o2
        ms = jnp.mean(att * att, axis=1, keepdims=True)
        o_ref[0] = att * lax.rsqrt(ms + LN_EPS) * g_ref[...] * (1.0 - LAM_INIT)


def _attn_sample(page_table, q, k_new, v_new2, cache_kt, cache_v2, lam_p, subln_g):
    nb, n_pages = page_table.shape
    pps = PAGES_PER_STEP
    page = cache_kt.shape[3]
    qb = jnp.broadcast_to(q[..., None], (*q.shape, page))
    n_steps = n_pages // pps
    kspec = lambda i: pl.BlockSpec((1, N_SLOTS, QK_DIM, page),
                                   lambda b, s, pt: (pt[b, s * pps + i], 0, 0, 0))
    vspec = lambda i: pl.BlockSpec((1, page * N_ATT_HEADS, ATT_HEAD_DIM),
                                   lambda b, s, pt: (pt[b, s * pps + i], 0, 0))
    grid_spec = pltpu.PrefetchScalarGridSpec(
        num_scalar_prefetch=1,
        grid=(nb, n_steps),
        in_specs=[
            pl.BlockSpec((1, N_SLOTS, QK_DIM), lambda b, s, pt: (b, 0, 0)),
            pl.BlockSpec((1, N_SLOTS, QK_DIM, page), lambda b, s, pt: (b, 0, 0, 0)),
            pl.BlockSpec((1, N_SLOTS, QK_DIM), lambda b, s, pt: (b, 0, 0)),
            pl.BlockSpec((1, N_SLOTS, ATT_HEAD_DIM), lambda b, s, pt: (b, 0, 0)),
            *[kspec(i) for i in range(pps)],
            *[vspec(i) for i in range(pps)],
            pl.BlockSpec((4, QK_DIM), lambda b, s, pt: (0, 0)),
            pl.BlockSpec((1, ATT_HEAD_DIM), lambda b, s, pt: (0, 0)),
        ],
        out_specs=pl.BlockSpec((1, N_ATT_HEADS, ATT_HEAD_DIM), lambda b, s, pt: (b, 0, 0)),
        scratch_shapes=[
            pltpu.VMEM((N_SLOTS, LANES), F32),
            pltpu.VMEM((N_SLOTS, LANES), F32),
            pltpu.VMEM((N_SLOTS, ATT_HEAD_DIM), F32),
            pltpu.VMEM((N_SLOTS, ATT_HEAD_DIM), F32),
        ],
    )
    return pl.pallas_call(
        functools.partial(_attn_sample_kernel, n_steps=n_steps, pps=pps, page=page),
        grid_spec=grid_spec,
        out_shape=jax.ShapeDtypeStruct((nb, N_ATT_HEADS, ATT_HEAD_DIM), F32),
        compiler_params=_cparams(("parallel", "arbitrary")),
        name="diff_attn_sample",
    )(page_table, q, qb, k_new, v_new2, *([cache_kt] * pps), *([cache_v2] * pps), lam_p, subln_g)


def _outproj_kernel(rec_ref, att_ref, wr_ref, wa_ref, x_ref, g_ref, b_ref, o_ref, ob_ref):
    mix = jnp.dot(rec_ref[...], wr_ref[...], preferred_element_type=F32)
    mix = mix + jnp.dot(att_ref[...], wa_ref[...], preferred_element_type=F32)
    y = _layer_norm(ALPHA * x_ref[...] + mix, g_ref[...], b_ref[...])
    o_ref[...] = y
    ob_ref[...] = y.astype(BF16)


def _outproj(rec_o, att_o, w_rec, w_att, x, g, b, *, tm):
    n = x.shape[0]
    return pl.pallas_call(
        _outproj_kernel,
        grid=(n // tm,),
        in_specs=[
            pl.BlockSpec((tm, D_REC), lambda i: (i, 0)),
            pl.BlockSpec((tm, D_ATT), lambda i: (i, 0)),
            pl.BlockSpec((D_REC, D_MODEL), lambda i: (0, 0)),
            pl.BlockSpec((D_ATT, D_MODEL), lambda i: (0, 0)),
            pl.BlockSpec((tm, D_MODEL), lambda i: (i, 0)),
            pl.BlockSpec((1, D_MODEL), lambda i: (0, 0)),
            pl.BlockSpec((1, D_MODEL), lambda i: (0, 0)),
        ],
        out_specs=[
            pl.BlockSpec((tm, D_MODEL), lambda i: (i, 0)),
            pl.BlockSpec((tm, D_MODEL), lambda i: (i, 0)),
        ],
        out_shape=[
            jax.ShapeDtypeStruct((n, D_MODEL), F32),
            jax.ShapeDtypeStruct((n, D_MODEL), BF16),
        ],
        compiler_params=_cparams(("parallel",)),
        name="out_proj_ln",
    )(rec_o, att_o, w_rec, w_att, x, g, b)


def _peer_scores_kernel(x_ref, wq_ref, k1_ref, k2_ref, o_ref):
    q = jnp.dot(x_ref[...], wq_ref[...], preferred_element_type=F32).astype(BF16)
    for h in range(PEER_HEADS):
        for c, k_ref in enumerate((k1_ref, k2_ref)):
            qb = q[:, (2 * h + c) * N_KEYS:(2 * h + c + 1) * N_KEYS]
            base = (c * PEER_HEADS + h) * N_KEYS
            o_ref[base:base + N_KEYS, :] = _nt_dot(k_ref[h], qb)


def _peer_scores(x1_bf, wq_bf, k1_bf, k2_bf, *, tm):
    n = x1_bf.shape[0]
    rows = 2 * PEER_HEADS * N_KEYS
    return pl.pallas_call(
        _peer_scores_kernel,
        grid=(n // tm,),
        in_specs=[
            pl.BlockSpec((tm, D_MODEL), lambda i: (i, 0)),
            pl.BlockSpec((D_MODEL, rows), lambda i: (0, 0)),
            pl.BlockSpec((PEER_HEADS, N_KEYS, N_KEYS), lambda i: (0, 0, 0)),
            pl.BlockSpec((PEER_HEADS, N_KEYS, N_KEYS), lambda i: (0, 0, 0)),
        ],
        out_specs=pl.BlockSpec((rows, tm), lambda i: (0, i)),
        out_shape=jax.ShapeDtypeStruct((rows, n), F32),
        compiler_params=_cparams(("parallel",)),
        name="peer_scores",
    )(x1_bf, wq_bf, k1_bf, k2_bf)


def _top_values(x, k):
    vals = []
    for _ in range(k):
        cur = jnp.max(x, axis=0, keepdims=True)
        vals.append(cur)
        x = jnp.where(x == cur, -jnp.inf, x)
    return jnp.concatenate(vals, axis=0)


def _peer_stats_kernel(s_ref, e1_ref, e2_ref, cnt_ref, rank_ref):
    half = PEER_HEADS * N_KEYS
    k = PEER_TOPK
    for h in range(PEER_HEADS):
        r1 = slice(h * N_KEYS, (h + 1) * N_KEYS)
        s1 = s_ref[r1, :]
        s2 = s_ref[half + h * N_KEYS:half + (h + 1) * N_KEYS, :]
        v1 = _top_values(s1, k)
        v2 = _top_values(s2, k)
        sub = lax.broadcasted_iota(jnp.int32, (8, 1), 0)
        cands = [v1[0:8] + v2[0:1], v1[8:16] + v2[0:1], v2[8:16] + v1[0:1]]
        for b in range(1, 8):
            n_a = k // (b + 1)
            cands.append(jnp.where(sub < n_a, v1[0:8] + v2[b:b + 1], -jnp.inf))
        cand = jnp.concatenate(cands, axis=0)
        work = cand
        theta = None
        for _ in range(k):
            theta = jnp.max(work, axis=0, keepdims=True)
            work = jnp.where(work == theta, -jnp.inf, work)
        m = v1[0:1] + v2[0:1]
        z = jnp.sum(jnp.where(cand >= theta, jnp.exp(cand - m), 0.0), axis=0, keepdims=True)
        e1_ref[r1, :] = jnp.exp(s1 - (v1[0:1] + jnp.log(z)))
        e2_ref[r1, :] = jnp.exp(s2 - v2[0:1]).astype(BF16)
        cnt = jnp.zeros(s1.shape, F32)
        rank = jnp.zeros(s2.shape, F32)
        for b in range(k):
            cnt = jnp.where(s1 + v2[b:b + 1] >= theta, b + 1.0, cnt)
            rank = jnp.where(v2[b:b + 1] > s2, b + 1.0, rank)
        cnt_ref[r1, :] = cnt
        rank_ref[r1, :] = rank.astype(BF16)


def _peer_stats(st, *, tl):
    rows, n = st.shape
    half = rows // 2
    spec = pl.BlockSpec((half, tl), lambda i: (0, i))
    f32, bf16 = jax.ShapeDtypeStruct((half, n), F32), jax.ShapeDtypeStruct((half, n), BF16)
    return pl.pallas_call(
        _peer_stats_kernel,
        grid=(n // tl,),
        in_specs=[pl.BlockSpec((rows, tl), lambda i: (0, i))],
        out_specs=[spec] * 4,
        out_shape=[f32, bf16, f32, bf16],
        compiler_params=_cparams(("parallel",)),
        name="peer_stats",
    )(st)


PEER_CHUNK = 8 * N_KEYS
PEER_SUB = 4 * N_KEYS
GATE_ROWS = 64


def _peer_dense_kernel(xt_ref, u_ref, vt_ref, rank_ref, e2_ref, e1_ref, cnt_ref, x_ref, g_ref,
                       b_ref, o_ref, acc_ref, sc_ref, p_ref, *, n_chunks):
    j = pl.program_id(1)
    tb = p_ref.shape[1]

    @pl.when(j == 0)
    def _():
        acc_ref[...] = jnp.zeros_like(acc_ref)

    xt = xt_ref[...]
    n_sub = PEER_CHUNK // PEER_SUB
    subs = [slice(c * PEER_SUB, (c + 1) * PEER_SUB) for c in range(n_sub)]
    sc_ref[subs[0], :] = jnp.dot(u_ref[subs[0], :], xt, preferred_element_type=F32)
    for c in range(n_sub):
        if c + 1 < n_sub:
            sc_ref[subs[c + 1], :] = jnp.dot(u_ref[subs[c + 1], :], xt, preferred_element_type=F32)
        for r in range(c * (PEER_SUB // N_KEYS), (c + 1) * (PEER_SUB // N_KEYS)):
            for i2h, lg in itertools.product(range(N_KEYS // GATE_ROWS), range(tb // LANES)):
                cols = slice(lg * LANES, (lg + 1) * LANES)
                rows = slice(r * N_KEYS + i2h * GATE_ROWS, r * N_KEYS + (i2h + 1) * GATE_ROWS)
                w = None
                for h in range(PEER_HEADS):
                    hr = slice(h * N_KEYS + i2h * GATE_ROWS, h * N_KEYS + (i2h + 1) * GATE_ROWS)
                    cnt = cnt_ref[h, r:r + 1, cols].astype(BF16)
                    e1 = e1_ref[h, r:r + 1, cols].astype(BF16)
                    gate = jnp.where(rank_ref[hr, cols] < cnt, e2_ref[hr, cols] * e1,
                                     jnp.zeros((), BF16))
                    w = gate if w is None else w + gate
                p_ref[rows, cols] = w * _gelu(sc_ref[rows, cols]).astype(BF16)
        acc_ref[...] += jnp.dot(vt_ref[0, :, subs[c]], p_ref[subs[c], :],
                                preferred_element_type=F32)

    @pl.when(j == n_chunks - 1)
    def _():
        o_ref[...] = _layer_norm(ALPHA * x_ref[...] + acc_ref[...].T, g_ref[...], b_ref[...])


def _peer_dense(x1t_bf, u_bf, vt_bf, e1t, e2t, cntt, rankt, x1, g, b, *, tb):
    n = x1.shape[0]
    e = PEER_CHUNK
    n_chunks = N_EXPERTS // e
    rpc = e // N_KEYS
    half = PEER_HEADS * N_KEYS
    tok = pl.BlockSpec((half, tb), lambda i, j: (0, i))
    by_head = lambda a: a.reshape(PEER_HEADS, N_KEYS, n)
    rowspec = pl.BlockSpec((PEER_HEADS, rpc, tb), lambda i, j: (0, j, i))
    return pl.pallas_call(
        functools.partial(_peer_dense_kernel, n_chunks=n_chunks),
        grid=(n // tb, n_chunks),
        in_specs=[
            pl.BlockSpec((D_MODEL, tb), lambda i, j: (0, i)),
            pl.BlockSpec((e, D_MODEL), lambda i, j: (j, 0)),
            pl.BlockSpec((1, D_MODEL, e), lambda i, j: (j, 0, 0)),
            tok, tok,
            rowspec, rowspec,
            pl.BlockSpec((tb, D_MODEL), lambda i, j: (i, 0), pipeline_mode=pl.Buffered(1)),
            pl.BlockSpec((1, D_MODEL), lambda i, j: (0, 0)),
            pl.BlockSpec((1, D_MODEL), lambda i, j: (0, 0)),
        ],
        out_specs=pl.BlockSpec((tb, D_MODEL), lambda i, j: (i, 0)),
        out_shape=jax.ShapeDtypeStruct((n, D_MODEL), F32),
        scratch_shapes=[
            pltpu.VMEM((D_MODEL, tb), F32),
            pltpu.VMEM((e, tb), F32),
            pltpu.VMEM((e, tb), BF16),
        ],
        compiler_params=_cparams(("parallel", "arbitrary")),
        name="peer_dense",
    )(x1t_bf, u_bf, vt_bf, rankt, e2t, by_head(e1t), by_head(cntt), x1, g, b)


def _rope_tables(pos):
    half = QK_DIM // 2
    inv = ROPE_THETA ** (-jnp.arange(half, dtype=F32) * 2.0 / QK_DIM)
    ang = pos.astype(F32)[:, None] * inv[None, :]
    cos, sin = jnp.cos(ang), jnp.sin(ang)
    reps = LANES // QK_DIM
    return (jnp.tile(jnp.concatenate([cos, cos], axis=1), (1, reps)),
            jnp.tile(jnp.concatenate([-sin, sin], axis=1), (1, reps)))


def _peer_mixer(x1, x1_bf, w, *, tm, tl, tb):
    st = _peer_scores(x1_bf, w["wq"], w["k1"], w["k2"], tm=tm)
    e1t, e2t, cntt, rankt = _peer_stats(st, tl=tl)
    return _peer_dense(x1_bf.T, w["u"], w["vt"], e1t, e2t, cntt, rankt, x1, w["ln2_g"], w["ln2_b"],
                       tb=tb)


def kernel(x_prompt, x_sample, cache_k, cache_v, state_conv, state_h, page_table, w_in, conv_w, conv_b, lru_wa, lru_ba, lru_wx, lru_bx, lru_lambda, lambda_q1, lambda_k1, lambda_q2, lambda_k2, subln_g, w_out, ln1_g, ln1_b, peer_wq, peer_k1, peer_k2, peer_u, peer_v, ln2_g, ln2_b):
    bsz, seq, _ = x_prompt.shape
    dbsz, dseq, _ = x_sample.shape
    n_pages = page_table.shape[1]
    page = cache_k.shape[2]
    past = n_pages * page
    n = bsz * seq
    l = 0

    w_in_bf = w_in[l].astype(BF16)
    w_rec = w_in_bf[:, :2 * D_REC]
    w_q = w_in_bf[:, 2 * D_REC:2 * D_REC + D_ATT]
    w_k = w_in_bf[:, 2 * D_REC + D_ATT:2 * D_REC + 2 * D_ATT]
    w_v = w_in_bf[:, 2 * D_REC + 2 * D_ATT:]
    w_out_bf = w_out[l].astype(BF16)
    wa_bf, wx_bf = lru_wa[l].astype(BF16), lru_wx[l].astype(BF16)
    row = lambda a: a[l].reshape(1, -1)
    lam_p = jnp.stack([lambda_q1[l], lambda_k1[l], lambda_q2[l], lambda_k2[l]])
    peer_w = dict(wq=peer_wq[l].astype(BF16), k1=peer_k1[l].astype(BF16), k2=peer_k2[l].astype(BF16),
                  u=peer_u[l].astype(BF16),
                  vt=peer_v[l].reshape(N_EXPERTS // PEER_CHUNK, PEER_CHUNK, D_MODEL)
                  .transpose(0, 2, 1).astype(BF16),
                  ln2_g=row(ln2_g), ln2_b=row(ln2_b))

    xp = x_prompt.reshape(n, D_MODEL)
    xp_bf = xp.astype(BF16)
    cos_p, sin_p = _rope_tables(jnp.arange(seq, dtype=jnp.int32))
    proj = functools.partial(_proj, xp_bf, cos_t=cos_p, sin_t=sin_p, tm=1024, tn=512)
    rec_p = proj(w_rec, rope=False, out_dtype=F32)
    q_p = proj(w_q, rope=True, out_dtype=BF16)
    k_p = proj(w_k, rope=True, out_dtype=F32)
    v_p = proj(w_v, rope=False, out_dtype=F32)
    rec_out_p, h_p = _rec_prompt(rec_p, conv_w[l], row(conv_b), wa_bf, wx_bf, row(lru_ba),
                                 row(lru_bx), row(lru_lambda), bsz=bsz, t=seq, cb=256)
    att_p = _attn_prompt(q_p, k_p, v_p, lam_p, row(subln_g), bsz=bsz, t=seq, tq=512)
    x1_p, x1_p_bf = _outproj(rec_out_p, att_p, w_out_bf[:D_REC], w_out_bf[D_REC:], xp,
                             row(ln1_g), row(ln1_b), tm=256)
    y_p = _peer_mixer(x1_p, x1_p_bf, peer_w, tm=256, tl=256, tb=512)

    xs = x_sample.reshape(dbsz * dseq, D_MODEL)
    ns = xs.shape[0]
    xs_bf = xs.astype(BF16)
    cos_s, sin_s = _rope_tables(jnp.full((ns,), past, dtype=jnp.int32))
    proj_s = functools.partial(_proj, xs_bf, cos_t=cos_s, sin_t=sin_s, tm=ns, tn=512)
    rec_s = proj_s(w_rec, rope=False, out_dtype=F32)
    q_s = proj_s(w_q, rope=True, out_dtype=F32)
    k_s = proj_s(w_k, rope=True, out_dtype=F32)
    v_s = proj_s(w_v, rope=False, out_dtype=F32)
    rec_out_s, conv_s_t, h_s = _rec_sample(
        rec_s, state_conv[l].transpose(1, 0, 2), state_h[l], conv_w[l], row(conv_b), wa_bf, wx_bf,
        row(lru_ba), row(lru_bx), row(lru_lambda))
    n_pool = cache_k.shape[1]
    att_s = _attn_sample(
        page_table, q_s.reshape(ns, N_SLOTS, QK_DIM), k_s.reshape(ns, N_SLOTS, QK_DIM),
        jnp.repeat(v_s.reshape(ns, N_ATT_HEADS, ATT_HEAD_DIM), 2, axis=1),
        cache_k[l].transpose(0, 2, 3, 1),
        cache_v[l].reshape(n_pool, page * N_ATT_HEADS, ATT_HEAD_DIM), lam_p, row(subln_g))
    pad = LANES - ns
    padr = lambda a: jnp.pad(a, ((0, pad), (0, 0)))
    x1_s, x1_s_bf = _outproj(padr(rec_out_s), padr(att_s.reshape(ns, D_ATT).astype(BF16)), w_out_bf[:D_REC],
                             w_out_bf[D_REC:], padr(xs), row(ln1_g), row(ln1_b), tm=LANES)
    y_s = _peer_mixer(x1_s, x1_s_bf, peer_w, tm=LANES, tl=LANES, tb=LANES)[:ns]

    k4 = (2 * N_ATT_HEADS, QK_DIM)
    v4 = (N_ATT_HEADS, ATT_HEAD_DIM)
    return (
        y_p.reshape(bsz, seq, D_MODEL),
        y_s.reshape(dbsz, dseq, D_MODEL),
        k_p.reshape(1, bsz, seq, *k4),
        v_p.reshape(1, bsz, seq, *v4),
        rec_p.reshape(bsz, seq, 2 * D_REC)[:, seq - (CONV_W - 1):, :D_REC][None],
        h_p.reshape(1, bsz, D_REC),
        k_s.reshape(1, dbsz, dseq, *k4),
        v_s.reshape(1, dbsz, dseq, *v4),
        conv_s_t.transpose(1, 0, 2)[None],
        h_s.reshape(1, dbsz, D_REC),
    )
```

```python
import functools
import itertools
import math

import jax
import jax.numpy as jnp
from jax import lax
from jax.experimental import pallas as pl
from jax.experimental.pallas import tpu as pltpu

F32 = jnp.float32
BF16 = jnp.bfloat16

D_MODEL = 2048
D_REC = 1024
D_ATT = 1024
N_REC_BLOCKS = 8
REC_BLOCK = 128
CONV_W = 4
LRU_C = 8.0
N_ATT_HEADS = 8
ATT_HEAD_DIM = 128
QK_DIM = 64
ROPE_THETA = 10000.0
N_KEYS = 128
N_EXPERTS = N_KEYS * N_KEYS
PEER_HEADS = 8
PEER_TOPK = 16
LN_EPS = 1e-5
NEG_INF = -1e30
DEPTH = 1
ALPHA = (2.0 * DEPTH) ** 0.25
LAM_INIT = 0.8 - 0.6 * math.exp(-0.3 * 0)
LANES = 128

VMEM_LIMIT = 56 * 1024 * 1024


def _cparams(sem, flags=None):
    return pltpu.CompilerParams(dimension_semantics=sem, vmem_limit_bytes=VMEM_LIMIT, flags=flags)


def _gelu(x):
    c = math.sqrt(2.0 / math.pi)
    return x * (0.5 * (1.0 + jnp.tanh(c * (x + 0.044715 * (x * x * x)))))


def _layer_norm(y, g, b):
    mu = jnp.mean(y, axis=-1, keepdims=True)
    yc = y - mu
    var = jnp.mean(yc * yc, axis=-1, keepdims=True)
    return yc * lax.rsqrt(var + LN_EPS) * g + b


def _proj_kernel(x_ref, w_ref, cos_ref, sin_ref, o_ref, *, rope):
    acc = jnp.dot(x_ref[...], w_ref[...], preferred_element_type=F32)
    if rope:
        c = cos_ref[...]
        s = sin_ref[...]
        lane = lax.broadcasted_iota(jnp.int32, (1, LANES), 1)
        first_half = (lane % QK_DIM) < (QK_DIM // 2)
        outs = []
        for j in range(acc.shape[1] // LANES):
            xc = acc[:, j * LANES:(j + 1) * LANES]
            fwd = pltpu.roll(xc, LANES - QK_DIM // 2, axis=1)
            bwd = pltpu.roll(xc, QK_DIM // 2, axis=1)
            partner = jnp.where(first_half, fwd, bwd)
            outs.append(xc * c + partner * s)
        acc = jnp.concatenate(outs, axis=1)
    o_ref[...] = acc.astype(o_ref.dtype)


def _proj(x_bf, w_bf, cos_t, sin_t, *, rope, out_dtype, tm, tn):
    n, k = x_bf.shape
    m = w_bf.shape[1]
    n_pos_blocks = cos_t.shape[0] // tm
    return pl.pallas_call(
        functools.partial(_proj_kernel, rope=rope),
        grid=(n // tm, m // tn),
        in_specs=[
            pl.BlockSpec((tm, k), lambda i, j: (i, 0)),
            pl.BlockSpec((k, tn), lambda i, j: (0, j)),
            pl.BlockSpec((tm, LANES), lambda i, j: (i % n_pos_blocks, 0)),
            pl.BlockSpec((tm, LANES), lambda i, j: (i % n_pos_blocks, 0)),
        ],
        out_specs=pl.BlockSpec((tm, tn), lambda i, j: (i, j)),
        out_shape=jax.ShapeDtypeStruct((n, m), out_dtype),
        compiler_params=_cparams(("parallel", "parallel")),
        name="in_proj_rope" if rope else "in_proj",
    )(x_bf, w_bf, cos_t, sin_t)


def _lru_gates(conv, wa_ref, wx_ref, ba, bx, lam):
    rs, is_ = [], []
    for blk in range(conv.shape[1] // REC_BLOCK):
        cb = conv[:, blk * REC_BLOCK:(blk + 1) * REC_BLOCK].astype(BF16)
        rs.append(jnp.dot(cb, wa_ref[blk], preferred_element_type=F32))
        is_.append(jnp.dot(cb, wx_ref[blk], preferred_element_type=F32))
    r = jax.nn.sigmoid(jnp.concatenate(rs, axis=1) + ba)
    i = jax.nn.sigmoid(jnp.concatenate(is_, axis=1) + bx)
    softplus_neg_lam = jnp.maximum(-lam, 0.0) + jnp.log1p(jnp.exp(-jnp.abs(lam)))
    log_a = -LRU_C * r * softplus_neg_lam
    a = jnp.exp(log_a)
    b = jnp.sqrt(1.0 - jnp.exp(2.0 * log_a)) * (i * conv)
    return a, b


def _shift_rows(x, s, fill):
    row = lax.broadcasted_iota(jnp.int32, x.shape, 0)
    return jnp.where(row >= s, pltpu.roll(x, s, axis=0), fill)


SCAN_GROUP = 8


def _rec_prompt_kernel(x_ref, g_ref, cw_ref, cb_ref, wa_ref, wx_ref, ba_ref, bx_ref, lam_ref,
                       o_ref, h_ref, a_sc, b_sc):
    x = x_ref[...]
    t = x.shape[0]
    cw = cw_ref[...]
    conv = (cb_ref[...] + cw[3:4] * x + cw[2:3] * _shift_rows(x, 1, 0.0)
            + cw[1:2] * _shift_rows(x, 2, 0.0) + cw[0:1] * _shift_rows(x, 3, 0.0))
    a, b = _lru_gates(conv, wa_ref, wx_ref, ba_ref[...], bx_ref[...], lam_ref[...])
    in_tile = lax.broadcasted_iota(jnp.int32, a.shape, 0) % SCAN_GROUP
    s = 1
    while s < SCAN_GROUP:
        b = a * jnp.where(in_tile >= s, pltpu.roll(b, s, axis=0), 0.0) + b
        a = a * jnp.where(in_tile >= s, pltpu.roll(a, s, axis=0), 1.0)
        s *= 2
    a_sc[...] = a
    b_sc[...] = b

    def tile(g, carry):
        rows = pl.ds(pl.multiple_of(g * SCAN_GROUP, SCAN_GROUP), SCAN_GROUP)
        h = a_sc[rows, :] * carry + b_sc[rows, :]
        b_sc[rows, :] = h
        return h[SCAN_GROUP - 1:SCAN_GROUP, :]

    h_last = lax.fori_loop(0, t // SCAN_GROUP, tile, jnp.zeros((1, x.shape[1]), F32), unroll=8)
    o_ref[...] = (b_sc[...] * _gelu(g_ref[...])).astype(o_ref.dtype)
    h_ref[0] = h_last


def _rec_prompt(rec, conv_w, conv_b, wa_bf, wx_bf, ba, bx, lam, *, bsz, t, cb):
    ncb = D_REC // cb
    return pl.pallas_call(
        _rec_prompt_kernel,
        grid=(bsz, ncb),
        in_specs=[
            pl.BlockSpec((t, cb), lambda b, c: (b, c)),
            pl.BlockSpec((t, cb), lambda b, c: (b, ncb + c)),
            pl.BlockSpec((CONV_W, cb), lambda b, c: (0, c)),
            pl.BlockSpec((1, cb), lambda b, c: (0, c)),
            pl.BlockSpec((cb // REC_BLOCK, REC_BLOCK, REC_BLOCK), lambda b, c: (c, 0, 0)),
            pl.BlockSpec((cb // REC_BLOCK, REC_BLOCK, REC_BLOCK), lambda b, c: (c, 0, 0)),
            pl.BlockSpec((1, cb), lambda b, c: (0, c)),
            pl.BlockSpec((1, cb), lambda b, c: (0, c)),
            pl.BlockSpec((1, cb), lambda b, c: (0, c)),
        ],
        out_specs=[
            pl.BlockSpec((t, cb), lambda b, c: (b, c)),
            pl.BlockSpec((1, 1, cb), lambda b, c: (b, 0, c)),
        ],
        out_shape=[
            jax.ShapeDtypeStruct((bsz * t, D_REC), BF16),
            jax.ShapeDtypeStruct((bsz, 1, D_REC), F32),
        ],
        scratch_shapes=[pltpu.VMEM((t, cb), F32), pltpu.VMEM((t, cb), F32)],
        compiler_params=_cparams(("parallel", "parallel")),
        name="rglru_prompt",
    )(rec, rec, conv_w, conv_b, wa_bf, wx_bf, ba, bx, lam)


def _rec_sample_kernel(rec_ref, sc_ref, h0_ref, cw_ref, cb_ref, wa_ref, wx_ref, ba_ref, bx_ref,
                       lam_ref, o_ref, nc_ref, h_ref):
    x = rec_ref[:, :D_REC]
    g = rec_ref[:, D_REC:]
    cw = cw_ref[...]
    conv = (cb_ref[...] + cw[0:1] * sc_ref[0] + cw[1:2] * sc_ref[1] + cw[2:3] * sc_ref[2]
            + cw[3:4] * x)
    a, b = _lru_gates(conv, wa_ref, wx_ref, ba_ref[...], bx_ref[...], lam_ref[...])
    h = a * h0_ref[...] + b
    o_ref[...] = (h * _gelu(g)).astype(o_ref.dtype)
    nc_ref[0] = sc_ref[1]
    nc_ref[1] = sc_ref[2]
    nc_ref[2] = x
    h_ref[...] = h


def _rec_sample(rec, sc_t, h0, conv_w, conv_b, wa_bf, wx_bf, ba, bx, lam):
    n = rec.shape[0]
    return pl.pallas_call(
        _rec_sample_kernel,
        out_shape=[
            jax.ShapeDtypeStruct((n, D_REC), BF16),
            jax.ShapeDtypeStruct((CONV_W - 1, n, D_REC), F32),
            jax.ShapeDtypeStruct((n, D_REC), F32),
        ],
        compiler_params=pltpu.CompilerParams(vmem_limit_bytes=VMEM_LIMIT),
        name="rglru_sample",
    )(rec, sc_t, h0, conv_w, conv_b, wa_bf, wx_bf, ba, bx, lam)


def _diff_lambda(lp):
    d1 = jnp.sum(lp[0:1] * lp[1:2], axis=1, keepdims=True)
    d2 = jnp.sum(lp[2:3] * lp[3:4], axis=1, keepdims=True)
    return jnp.exp(d1) - jnp.exp(d2) + LAM_INIT


def _nt_dot(a, b):
    return lax.dot_general(a, b, (((1,), (1,)), ((), ())), preferred_element_type=F32)


def _attn_prompt_kernel(q_ref, k_ref, v_ref, lp_ref, g_ref, o_ref, m_sc, l_sc, acc_sc, *, tq):
    qi = pl.program_id(2)
    lane = lax.broadcasted_iota(jnp.int32, (1, LANES), 1)
    qs = q_ref[...] * (QK_DIM ** -0.5)
    zero = jnp.zeros_like(qs)
    qmaps = (jnp.where(lane < QK_DIM, qs, zero), jnp.where(lane >= QK_DIM, qs, zero))
    m_sc[...] = jnp.full(m_sc.shape, NEG_INF, F32)
    l_sc[...] = jnp.zeros_like(l_sc)
    acc_sc[...] = jnp.zeros_like(acc_sc)
    reps = tq // LANES

    def block(kb, masked):
        start = pl.multiple_of(kb * tq, tq)
        k = k_ref[pl.ds(start, tq), :].astype(BF16)
        v = v_ref[pl.ds(start, tq), :].astype(BF16)
        for c in range(2):
            s = _nt_dot(qmaps[c], k)
            if masked:
                row = lax.broadcasted_iota(jnp.int32, s.shape, 0)
                col = lax.broadcasted_iota(jnp.int32, s.shape, 1)
                s = jnp.where(col <= row, s, NEG_INF)
            m_old = m_sc[c]
            m_new = jnp.maximum(m_old, jnp.max(s, axis=1, keepdims=True))
            alpha = jnp.exp(m_old - m_new)
            p = jnp.exp(s - jnp.tile(m_new, (1, reps)))
            l_sc[c] = alpha * l_sc[c] + jnp.sum(p, axis=1, keepdims=True)
            acc_sc[c] = alpha * acc_sc[c] + jnp.dot(p.astype(BF16), v, preferred_element_type=F32)
            m_sc[c] = m_new

    @pl.loop(0, qi)
    def _(kb):
        block(kb, False)

    block(qi, True)
    lam = _diff_lambda(lp_ref[...])
    att = acc_sc[0] / l_sc[0] - lam * (acc_sc[1] / l_sc[1])
    ms = jnp.mean(att * att, axis=1, keepdims=True)
    att = att * lax.rsqrt(ms + LN_EPS) * g_ref[...] * (1.0 - LAM_INIT)
    o_ref[...] = att.astype(o_ref.dtype)


def _attn_prompt(q_bf, k, v, lam_p, subln_g, *, bsz, t, tq):
    nq = t // tq
    return pl.pallas_call(
        functools.partial(_attn_prompt_kernel, tq=tq),
        grid=(bsz, N_ATT_HEADS, nq),
        in_specs=[
            pl.BlockSpec((tq, LANES), lambda b, h, i: (b * nq + i, h)),
            pl.BlockSpec((t, LANES), lambda b, h, i: (b, h)),
            pl.BlockSpec((t, LANES), lambda b, h, i: (b, h)),
            pl.BlockSpec((4, QK_DIM), lambda b, h, i: (0, 0)),
            pl.BlockSpec((1, ATT_HEAD_DIM), lambda b, h, i: (0, 0)),
        ],
        out_specs=pl.BlockSpec((tq, LANES), lambda b, h, i: (b * nq + i, h)),
        out_shape=jax.ShapeDtypeStruct((bsz * t, D_ATT), BF16),
        scratch_shapes=[
            pltpu.VMEM((2, tq, LANES), F32),
            pltpu.VMEM((2, tq, LANES), F32),
            pltpu.VMEM((2, tq, ATT_HEAD_DIM), F32),
        ],
        compiler_params=_cparams(("parallel", "parallel", "arbitrary")),
        name="diff_attn_prompt",
    )(q_bf, k, v, lam_p, subln_g)


N_SLOTS = 2 * N_ATT_HEADS
PAGES_PER_STEP = 16


def _attn_sample_kernel(pt_ref, q_ref, qb_ref, kn_ref, vn_ref, *refs, n_steps, pps, page):
    del pt_ref
    k_refs, v_refs = refs[:pps], refs[pps:2 * pps]
    lp_ref, g_ref, o_ref, m_sc, l_sc, acc_sc, o_sc = refs[2 * pps:]
    step = pl.program_id(1)
    q = q_ref[0] * (QK_DIM ** -0.5)

    @pl.when(step == 0)
    def _():
        s_self = jnp.sum(q * kn_ref[0], axis=1, keepdims=True)
        m_sc[...] = jnp.broadcast_to(s_self, m_sc.shape)
        l_sc[...] = jnp.ones_like(l_sc)
        acc_sc[...] = vn_ref[0]

    qb = qb_ref[0]
    slot_head = lax.broadcasted_iota(jnp.int32, (N_SLOTS, LANES), 0) // 2
    for i in range(pps):
        st = jnp.sum(k_refs[i][0] * qb, axis=1) * (QK_DIM ** -0.5)
        m_old = m_sc[...]
        m_new = jnp.maximum(m_old, jnp.max(st, axis=1, keepdims=True))
        alpha = jnp.exp(m_old - m_new)
        p = jnp.exp(st - m_new)
        l_sc[...] = alpha * l_sc[...] + jnp.sum(p, axis=1, keepdims=True)
        m_sc[...] = m_new
        p_bf = p.astype(BF16)
        pv = jnp.zeros((N_SLOTS, ATT_HEAD_DIM), F32)
        for h in range(N_ATT_HEADS):
            v_h = v_refs[i][0, pl.ds(h, page, stride=N_ATT_HEADS), :].astype(BF16)
            full = jnp.dot(p_bf, v_h, preferred_element_type=F32)
            pv = jnp.where(slot_head == h, full, pv)
        acc_sc[...] = alpha * acc_sc[...] + pv

    @pl.when(step == n_steps - 1)
    def _():
        lam = _diff_lambda(lp_ref[...])
        o_sc[...] = acc_sc[...] / l_sc[...]
        o1 = o_sc[pl.ds(0, N_ATT_HEADS, stride=2), :]
        o2 = o_sc[pl.ds(1, N_ATT_HEADS, stride=2), :]
        att = o1 - lam * o2
        ms = jnp.mean(att * att, axis=1, keepdims=True)
        o_ref[0] = att * lax.rsqrt(ms + LN_EPS) * g_ref[...] * (1.0 - LAM_INIT)


def _attn_sample(page_table, q, k_new, v_new2, cache_kt, cache_v2, lam_p, subln_g):
    nb, n_pages = page_table.shape
    pps = PAGES_PER_STEP
    page = cache_kt.shape[3]
    qb = jnp.broadcast_to(q[..., None], (*q.shape, page))
    n_steps = n_pages // pps
    kspec = lambda i: pl.BlockSpec((1, N_SLOTS, QK_DIM, page),
                                   lambda b, s, pt: (pt[b, s * pps + i], 0, 0, 0))
    vspec = lambda i: pl.BlockSpec((1, page * N_ATT_HEADS, ATT_HEAD_DIM),
                                   lambda b, s, pt: (pt[b, s * pps + i], 0, 0))
    grid_spec = pltpu.PrefetchScalarGridSpec(
        num_scalar_prefetch=1,
        grid=(nb, n_steps),
        in_specs=[
            pl.BlockSpec((1, N_SLOTS, QK_DIM), lambda b, s, pt: (b, 0, 0)),
            pl.BlockSpec((1, N_SLOTS, QK_DIM, page), lambda b, s, pt: (b, 0, 0, 0)),
            pl.BlockSpec((1, N_SLOTS, QK_DIM), lambda b, s, pt: (b, 0, 0)),
            pl.BlockSpec((1, N_SLOTS, ATT_HEAD_DIM), lambda b, s, pt: (b, 0, 0)),
            *[kspec(i) for i in range(pps)],
            *[vspec(i) for i in range(pps)],
            pl.BlockSpec((4, QK_DIM), lambda b, s, pt: (0, 0)),
            pl.BlockSpec((1, ATT_HEAD_DIM), lambda b, s, pt: (0, 0)),
        ],
        out_specs=pl.BlockSpec((1, N_ATT_HEADS, ATT_HEAD_DIM), lambda b, s, pt: (b, 0, 0)),
        scratch_shapes=[
            pltpu.VMEM((N_SLOTS, LANES), F32),
            pltpu.VMEM((N_SLOTS, LANES), F32),
            pltpu.VMEM((N_SLOTS, ATT_HEAD_DIM), F32),
            pltpu.VMEM((N_SLOTS, ATT_HEAD_DIM), F32),
        ],
    )
    return pl.pallas_call(
        functools.partial(_attn_sample_kernel, n_steps=n_steps, pps=pps, page=page),
        grid_spec=grid_spec,
        out_shape=jax.ShapeDtypeStruct((nb, N_ATT_HEADS, ATT_HEAD_DIM), F32),
        compiler_params=_cparams(("parallel", "arbitrary")),
        name="diff_attn_sample",
    )(page_table, q, qb, k_new, v_new2, *([cache_kt] * pps), *([cache_v2] * pps), lam_p, subln_g)


def _outproj_kernel(rec_ref, att_ref, wr_ref, wa_ref, x_ref, g_ref, b_ref, o_ref, ob_ref):
    mix = jnp.dot(rec_ref[...], wr_ref[...], preferred_element_type=F32)
    mix = mix + jnp.dot(att_ref[...], wa_ref[...], preferred_element_type=F32)
    y = _layer_norm(ALPHA * x_ref[...] + mix, g_ref[...], b_ref[...])
    o_ref[...] = y
    ob_ref[...] = y.astype(BF16)


def _outproj(rec_o, att_o, w_rec, w_att, x, g, b, *, tm):
    n = x.shape[0]
    return pl.pallas_call(
        _outproj_kernel,
        grid=(n // tm,),
        in_specs=[
            pl.BlockSpec((tm, D_REC), lambda i: (i, 0)),
            pl.BlockSpec((tm, D_ATT), lambda i: (i, 0)),
            pl.BlockSpec((D_REC, D_MODEL), lambda i: (0, 0)),
            pl.BlockSpec((D_ATT, D_MODEL), lambda i: (0, 0)),
            pl.BlockSpec((tm, D_MODEL), lambda i: (i, 0)),
            pl.BlockSpec((1, D_MODEL), lambda i: (0, 0)),
            pl.BlockSpec((1, D_MODEL), lambda i: (0, 0)),
        ],
        out_specs=[
            pl.BlockSpec((tm, D_MODEL), lambda i: (i, 0)),
            pl.BlockSpec((tm, D_MODEL), lambda i: (i, 0)),
        ],
        out_shape=[
            jax.ShapeDtypeStruct((n, D_MODEL), F32),
            jax.ShapeDtypeStruct((n, D_MODEL), BF16),
        ],
        compiler_params=_cparams(("parallel",)),
        name="out_proj_ln",
    )(rec_o, att_o, w_rec, w_att, x, g, b)


def _peer_scores_kernel(x_ref, wq_ref, k1_ref, k2_ref, o_ref):
    q = jnp.dot(x_ref[...], wq_ref[...], preferred_element_type=F32).astype(BF16)
    for h in range(PEER_HEADS):
        for c, k_ref in enumerate((k1_ref, k2_ref)):
            qb = q[:, (2 * h + c) * N_KEYS:(2 * h + c + 1) * N_KEYS]
            base = (c * PEER_HEADS + h) * N_KEYS
            o_ref[base:base + N_KEYS, :] = _nt_dot(k_ref[h], qb)


def _peer_scores(x1_bf, wq_bf, k1_bf, k2_bf, *, tm):
    n = x1_bf.shape[0]
    rows = 2 * PEER_HEADS * N_KEYS
    return pl.pallas_call(
        _peer_scores_kernel,
        grid=(n // tm,),
        in_specs=[
            pl.BlockSpec((tm, D_MODEL), lambda i: (i, 0)),
            pl.BlockSpec((D_MODEL, rows), lambda i: (0, 0)),
            pl.BlockSpec((PEER_HEADS, N_KEYS, N_KEYS), lambda i: (0, 0, 0)),
            pl.BlockSpec((PEER_HEADS, N_KEYS, N_KEYS), lambda i: (0, 0, 0)),
        ],
        out_specs=pl.BlockSpec((rows, tm), lambda i: (0, i)),
        out_shape=jax.ShapeDtypeStruct((rows, n), F32),
        compiler_params=_cparams(("parallel",)),
        name="peer_scores",
    )(x1_bf, wq_bf, k1_bf, k2_bf)


def _top_values(x, k):
    vals = []
    for _ in range(k):
        cur = jnp.max(x, axis=0, keepdims=True)
        vals.append(cur)
        x = jnp.where(x == cur, -jnp.inf, x)
    return jnp.concatenate(vals, axis=0)


def _peer_stats_kernel(s_ref, e1_ref, e2_ref, cnt_ref, rank_ref):
    half = PEER_HEADS * N_KEYS
    k = PEER_TOPK
    for h in range(PEER_HEADS):
        r1 = slice(h * N_KEYS, (h + 1) * N_KEYS)
        s1 = s_ref[r1, :]
        s2 = s_ref[half + h * N_KEYS:half + (h + 1) * N_KEYS, :]
        v1 = _top_values(s1, k)
        v2 = _top_values(s2, k)
        sub = lax.broadcasted_iota(jnp.int32, (8, 1), 0)
        cands = [v1[0:8] + v2[0:1], v1[8:16] + v2[0:1], v2[8:16] + v1[0:1]]
        for b in range(1, 8):
            n_a = k // (b + 1)
            cands.append(jnp.where(sub < n_a, v1[0:8] + v2[b:b + 1], -jnp.inf))
        cand = jnp.concatenate(cands, axis=0)
        work = cand
        theta = None
        for _ in range(k):
            theta = jnp.max(work, axis=0, keepdims=True)
            work = jnp.where(work == theta, -jnp.inf, work)
        m = v1[0:1] + v2[0:1]
        z = jnp.sum(jnp.where(cand >= theta, jnp.exp(cand - m), 0.0), axis=0, keepdims=True)
        e1_ref[r1, :] = jnp.exp(s1 - (v1[0:1] + jnp.log(z)))
        e2_ref[r1, :] = jnp.exp(s2 - v2[0:1]).astype(BF16)
        cnt = jnp.zeros(s1.shape, F32)
        rank = jnp.zeros(s2.shape, F32)
        for b in range(k):
            cnt = jnp.where(s1 + v2[b:b + 1] >= theta, b + 1.0, cnt)
            rank = jnp.where(v2[b:b + 1] > s2, b + 1.0, rank)
        cnt_ref[r1, :] = cnt
        rank_ref[r1, :] = rank.astype(BF16)


def _peer_stats(st, *, tl):
    rows, n = st.shape
    half = rows // 2
    spec = pl.BlockSpec((half, tl), lambda i: (0, i))
    f32, bf16 = jax.ShapeDtypeStruct((half, n), F32), jax.ShapeDtypeStruct((half, n), BF16)
    return pl.pallas_call(
        _peer_stats_kernel,
        grid=(n // tl,),
        in_specs=[pl.BlockSpec((rows, tl), lambda i: (0, i))],
        out_specs=[spec] * 4,
        out_shape=[f32, bf16, f32, bf16],
        compiler_params=_cparams(("parallel",)),
        name="peer_stats",
    )(st)


PEER_CHUNK = 8 * N_KEYS
PEER_SUB = 4 * N_KEYS
GATE_ROWS = 64


def _peer_gates_kernel(rank_ref, e2_ref, e1_ref, cnt_ref, w_ref):
    tb = w_ref.shape[1]
    for r in range(PEER_CHUNK // N_KEYS):
        for i2h, lg in itertools.product(range(N_KEYS // GATE_ROWS), range(tb // LANES)):
            cols = slice(lg * LANES, (lg + 1) * LANES)
            rows = slice(r * N_KEYS + i2h * GATE_ROWS, r * N_KEYS + (i2h + 1) * GATE_ROWS)
            w = None
            for h in range(PEER_HEADS):
                hr = slice(h * N_KEYS + i2h * GATE_ROWS, h * N_KEYS + (i2h + 1) * GATE_ROWS)
                cnt = cnt_ref[h, r:r + 1, cols].astype(BF16)
                e1 = e1_ref[h, r:r + 1, cols].astype(BF16)
                gate = jnp.where(rank_ref[hr, cols] < cnt, e2_ref[hr, cols] * e1,
                                 jnp.zeros((), BF16))
                w = gate if w is None else w + gate
            w_ref[rows, cols] = w


def _peer_gates(e1t, e2t, cntt, rankt, *, tb):
    n = e2t.shape[1]
    e = PEER_CHUNK
    rpc = e // N_KEYS
    half = PEER_HEADS * N_KEYS
    tok = pl.BlockSpec((half, tb), lambda i, j: (0, i))
    by_head = lambda a: a.reshape(PEER_HEADS, N_KEYS, n)
    rowspec = pl.BlockSpec((PEER_HEADS, rpc, tb), lambda i, j: (0, j, i))
    return pl.pallas_call(
        _peer_gates_kernel,
        grid=(n // tb, N_EXPERTS // e),
        in_specs=[tok, tok, rowspec, rowspec],
        out_specs=pl.BlockSpec((e, tb), lambda i, j: (j, i)),
        out_shape=jax.ShapeDtypeStruct((N_EXPERTS, n), BF16),
        compiler_params=_cparams(("parallel", "parallel")),
        name="peer_gates",
    )(rankt, e2t, by_head(e1t), by_head(cntt))


def _peer_dense_kernel(xt_ref, u_ref, vt_ref, w_ref, x_ref, g_ref, b_ref, o_ref, acc_ref, *,
                       n_chunks):
    j = pl.program_id(1)

    @pl.when(j == 0)
    def _():
        acc_ref[...] = jnp.zeros_like(acc_ref)

    sc = jnp.dot(u_ref[...], xt_ref[...], preferred_element_type=F32)
    p = w_ref[...] * _gelu(sc).astype(BF16)
    acc_ref[...] += jnp.dot(vt_ref[0], p, preferred_element_type=F32)

    @pl.when(j == n_chunks - 1)
    def _():
        o_ref[...] = _layer_norm(ALPHA * x_ref[...] + acc_ref[...].T, g_ref[...], b_ref[...])


def _peer_dense(x1t_bf, u_bf, vt_bf, e1t, e2t, cntt, rankt, x1, g, b, *, tb):
    n = x1.shape[0]
    e = PEER_CHUNK
    n_chunks = N_EXPERTS // e
    wt = _peer_gates(e1t, e2t, cntt, rankt, tb=tb)
    return pl.pallas_call(
        functools.partial(_peer_dense_kernel, n_chunks=n_chunks),
        grid=(n // tb, n_chunks),
        in_specs=[
            pl.BlockSpec((D_MODEL, tb), lambda i, j: (0, i)),
            pl.BlockSpec((e, D_MODEL), lambda i, j: (j, 0)),
            pl.BlockSpec((1, D_MODEL, e), lambda i, j: (j, 0, 0)),
            pl.BlockSpec((e, tb), lambda i, j: (j, i)),
            pl.BlockSpec((tb, D_MODEL), lambda i, j: (i, 0), pipeline_mode=pl.Buffered(1)),
            pl.BlockSpec((1, D_MODEL), lambda i, j: (0, 0)),
            pl.BlockSpec((1, D_MODEL), lambda i, j: (0, 0)),
        ],
        out_specs=pl.BlockSpec((tb, D_MODEL), lambda i, j: (i, 0)),
        out_shape=jax.ShapeDtypeStruct((n, D_MODEL), F32),
        scratch_shapes=[pltpu.VMEM((D_MODEL, tb), F32)],
        compiler_params=_cparams(("parallel", "arbitrary")),
        name="peer_dense",
    )(x1t_bf, u_bf, vt_bf, wt, x1, g, b)


def _rope_tables(pos):
    half = QK_DIM // 2
    inv = ROPE_THETA ** (-jnp.arange(half, dtype=F32) * 2.0 / QK_DIM)
    ang = pos.astype(F32)[:, None] * inv[None, :]
    cos, sin = jnp.cos(ang), jnp.sin(ang)
    reps = LANES // QK_DIM
    return (jnp.tile(jnp.concatenate([cos, cos], axis=1), (1, reps)),
            jnp.tile(jnp.concatenate([-sin, sin], axis=1), (1, reps)))


def _peer_mixer(x1, x1_bf, w, *, tm, tl, tb):
    st = _peer_scores(x1_bf, w["wq"], w["k1"], w["k2"], tm=tm)
    e1t, e2t, cntt, rankt = _peer_stats(st, tl=tl)
    return _peer_dense(x1_bf.T, w["u"], w["vt"], e1t, e2t, cntt, rankt, x1, w["ln2_g"], w["ln2_b"],
                       tb=tb)


def kernel(x_prompt, x_sample, cache_k, cache_v, state_conv, state_h, page_table, w_in, conv_w, conv_b, lru_wa, lru_ba, lru_wx, lru_bx, lru_lambda, lambda_q1, lambda_k1, lambda_q2, lambda_k2, subln_g, w_out, ln1_g, ln1_b, peer_wq, peer_k1, peer_k2, peer_u, peer_v, ln2_g, ln2_b):
    bsz, seq, _ = x_prompt.shape
    dbsz, dseq, _ = x_sample.shape
    n_pages = page_table.shape[1]
    page = cache_k.shape[2]
    past = n_pages * page
    n = bsz * seq
    l = 0

    w_in_bf = w_in[l].astype(BF16)
    w_rec = w_in_bf[:, :2 * D_REC]
    w_q = w_in_bf[:, 2 * D_REC:2 * D_REC + D_ATT]
    w_k = w_in_bf[:, 2 * D_REC + D_ATT:2 * D_REC + 2 * D_ATT]
    w_v = w_in_bf[:, 2 * D_REC + 2 * D_ATT:]
    w_out_bf = w_out[l].astype(BF16)
    wa_bf, wx_bf = lru_wa[l].astype(BF16), lru_wx[l].astype(BF16)
    row = lambda a: a[l].reshape(1, -1)
    lam_p = jnp.stack([lambda_q1[l], lambda_k1[l], lambda_q2[l], lambda_k2[l]])
    peer_w = dict(wq=peer_wq[l].astype(BF16), k1=peer_k1[l].astype(BF16), k2=peer_k2[l].astype(BF16),
                  u=peer_u[l].astype(BF16),
                  vt=peer_v[l].reshape(N_EXPERTS // PEER_CHUNK, PEER_CHUNK, D_MODEL)
                  .transpose(0, 2, 1).astype(BF16),
                  ln2_g=row(ln2_g), ln2_b=row(ln2_b))

    xp = x_prompt.reshape(n, D_MODEL)
    xp_bf = xp.astype(BF16)
    cos_p, sin_p = _rope_tables(jnp.arange(seq, dtype=jnp.int32))
    proj = functools.partial(_proj, xp_bf, cos_t=cos_p, sin_t=sin_p, tm=1024, tn=512)
    rec_p = proj(w_rec, rope=False, out_dtype=F32)
    q_p = proj(w_q, rope=True, out_dtype=BF16)
    k_p = proj(w_k, rope=True, out_dtype=F32)
    v_p = proj(w_v, rope=False, out_dtype=F32)
    rec_out_p, h_p = _rec_prompt(rec_p, conv_w[l], row(conv_b), wa_bf, wx_bf, row(lru_ba),
                                 row(lru_bx), row(lru_lambda), bsz=bsz, t=seq, cb=256)
    att_p = _attn_prompt(q_p, k_p, v_p, lam_p, row(subln_g), bsz=bsz, t=seq, tq=512)
    x1_p, x1_p_bf = _outproj(rec_out_p, att_p, w_out_bf[:D_REC], w_out_bf[D_REC:], xp,
                             row(ln1_g), row(ln1_b), tm=256)
    y_p = _peer_mixer(x1_p, x1_p_bf, peer_w, tm=256, tl=256, tb=512)

    xs = x_sample.reshape(dbsz * dseq, D_MODEL)
    ns = xs.shape[0]
    xs_bf = xs.astype(BF16)
    cos_s, sin_s = _rope_tables(jnp.full((ns,), past, dtype=jnp.int32))
    proj_s = functools.partial(_proj, xs_bf, cos_t=cos_s, sin_t=sin_s, tm=ns, tn=512)
    rec_s = proj_s(w_rec, rope=False, out_dtype=F32)
    q_s = proj_s(w_q, rope=True, out_dtype=F32)
    k_s = proj_s(w_k, rope=True, out_dtype=F32)
    v_s = proj_s(w_v, rope=False, out_dtype=F32)
    rec_out_s, conv_s_t, h_s = _rec_sample(
        rec_s, state_conv[l].transpose(1, 0, 2), state_h[l], conv_w[l], row(conv_b), wa_bf, wx_bf,
        row(lru_ba), row(lru_bx), row(lru_lambda))
    n_pool = cache_k.shape[1]
    att_s = _attn_sample(
        page_table, q_s.reshape(ns, N_SLOTS, QK_DIM), k_s.reshape(ns, N_SLOTS, QK_DIM),
        jnp.repeat(v_s.reshape(ns, N_ATT_HEADS, ATT_HEAD_DIM), 2, axis=1),
        cache_k[l].transpose(0, 2, 3, 1),
        cache_v[l].reshape(n_pool, page * N_ATT_HEADS, ATT_HEAD_DIM), lam_p, row(subln_g))
    pad = LANES - ns
    padr = lambda a: jnp.pad(a, ((0, pad), (0, 0)))
    x1_s, x1_s_bf = _outproj(padr(rec_out_s), padr(att_s.reshape(ns, D_ATT).astype(BF16)), w_out_bf[:D_REC],
                             w_out_bf[D_REC:], padr(xs), row(ln1_g), row(ln1_b), tm=LANES)
    y_s = _peer_mixer(x1_s, x1_s_bf, peer_w, tm=LANES, tl=LANES, tb=LANES)[:ns]

    k4 = (2 * N_ATT_HEADS, QK_DIM)
    v4 = (N_ATT_HEADS, ATT_HEAD_DIM)
    return (
        y_p.reshape(bsz, seq, D_MODEL),
        y_s.reshape(dbsz, dseq, D_MODEL),
        k_p.reshape(1, bsz, seq, *k4),
        v_p.reshape(1, bsz, seq, *v4),
        rec_p.reshape(bsz, seq, 2 * D_REC)[:, seq - (CONV_W - 1):, :D_REC][None],
        h_p.reshape(1, bsz, D_REC),
        k_s.reshape(1, dbsz, dseq, *k4),
        v_s.reshape(1, dbsz, dseq, *v4),
        conv_s_t.transpose(1, 0, 2)[None],
        h_s.reshape(1, dbsz, D_REC),
    )
```
